```python
import jax, jax.numpy as jnp
from jax import lax
import numpy as np

D_MODEL = 4096
BATCH = 2
SEQ = 8192
DEPTH = 1

HEAD_DIM = 128
ATTN_GROUPS = ((128, 1), (512, 4), (2048, 16))
N_ATTN_GROUPS = 3
ATTN_HEADS_PER_GROUP = 8
ATTN_WIDTH = N_ATTN_GROUPS * ATTN_HEADS_PER_GROUP * HEAD_DIM
ATTN_OUT_WIDTH = ATTN_HEADS_PER_GROUP * HEAD_DIM
ROPE_DIM = HEAD_DIM // 4
ROPE_THETA = 500000.0
MLSTM_WIDTH = D_MODEL // 2
MLSTM_HEADS = 8
MLSTM_HEAD_DIM = MLSTM_WIDTH // MLSTM_HEADS
MLSTM_CHUNK = 128
CONV_WIDTH = 5
D_FF = 11008
NORM_EPS = 1e-6
NEG_INF = -1e30
IN_SIZES = (ATTN_WIDTH, ATTN_WIDTH, ATTN_WIDTH, MLSTM_WIDTH, MLSTM_WIDTH, MLSTM_WIDTH, MLSTM_WIDTH, 4 * MLSTM_HEADS, D_MODEL, D_MODEL)
IN_WIDTH = sum(IN_SIZES)

kernel_name = "hybrid_dilated_attn_mlstm_macaron"


def rms_norm(x, g):
    xf = x.astype(jnp.float32)
    y = xf * lax.rsqrt(jnp.mean(xf * xf, axis=-1, keepdims=True) + NORM_EPS)
    return (y * g.astype(jnp.float32)).astype(x.dtype)


def swiglu(h, w_gate, w_up, w_down):
    return (jax.nn.silu(h @ w_gate) * (h @ w_up)) @ w_down


def split_cols(u, sizes):
    offs = np.cumsum(np.array(sizes))[:-1].tolist()
    return jnp.split(u, offs, axis=-1)


def partial_rotary(x, pos):
    half = ROPE_DIM // 2
    inv_freq = ROPE_THETA ** (-jnp.arange(half, dtype=jnp.float32) * 2.0 / ROPE_DIM)
    ang = pos.astype(jnp.float32)[:, None] * inv_freq[None, :]
    cos = jnp.cos(ang)[None, :, None, :]
    sin = jnp.sin(ang)[None, :, None, :]
    xf = x.astype(jnp.float32)
    x1 = xf[..., :half]
    x2 = xf[..., half:ROPE_DIM]
    out = jnp.concatenate([x1 * cos - x2 * sin, x2 * cos + x1 * sin, xf[..., ROPE_DIM:]], axis=-1)
    return out.astype(x.dtype)


def dilated_window_attention(q, k, v, window, dilation):
    T = q.shape[1]
    dh = q.shape[-1]
    reach = window // (2 * dilation)
    blk = reach
    span = dilation * blk
    Tp = -(-T // span) * span
    S = Tp // dilation
    nb = S // blk

    def to_blocks(a):
        a = jnp.pad(a, [(0, 0), (0, Tp - T)] + [(0, 0)] * (a.ndim - 2))
        a = a.reshape((a.shape[0], S, dilation) + a.shape[2:])
        a = jnp.moveaxis(a, 2, 1)
        return a.reshape((a.shape[0], dilation, nb, blk) + a.shape[3:])

    def with_halo(a):
        ap = jnp.pad(a, [(0, 0), (0, 0), (1, 1)] + [(0, 0)] * (a.ndim - 3))
        return jnp.concatenate([ap[:, :, :-2], ap[:, :, 1:-1], ap[:, :, 2:]], axis=3)

    def from_blocks(a):
        a = a.reshape((a.shape[0], dilation, S) + a.shape[4:])
        a = jnp.moveaxis(a, 1, 2)
        return a.reshape((a.shape[0], Tp) + a.shape[3:])[:, :T]

    qb = to_blocks(q.astype(jnp.float32))
    kh = with_halo(to_blocks(k.astype(jnp.float32)))
    vh = with_halo(to_blocks(v.astype(jnp.float32)))
    kvalid = with_halo(to_blocks(jnp.ones((1, T), dtype=bool)))
    s = jnp.einsum('brnqhd,brnkhd->brnhqk', qb, kh) * (dh ** -0.5)
    a_idx = jnp.arange(blk)[:, None]
    c_idx = jnp.arange(3 * blk)[None, :]
    band = jnp.abs(c_idx - blk - a_idx) <= reach
    mask = band & kvalid[:, :, :, None, None, :]
    s = jnp.where(mask, s, NEG_INF)
    m = jnp.max(s, axis=-1, keepdims=True)
    p = jnp.exp(s - m)
    den = jnp.sum(p, axis=-1, keepdims=True)
    o = jnp.einsum('brnhqk,brnkhd->brnqhd', p / den, vh)
    lse = jnp.swapaxes((m + jnp.log(den))[..., 0], 3, 4)
    return from_blocks(o), from_blocks(lse)


def dilated_attention_branch(q, k, v, pos):
    B, T, _ = q.shape
    shp = (B, T, N_ATTN_GROUPS * ATTN_HEADS_PER_GROUP, HEAD_DIM)
    q = partial_rotary(q.reshape(shp), pos)
    k = partial_rotary(k.reshape(shp), pos)
    v = v.reshape(shp)
    outs, lses = [], []
    for g, (window, dilation) in enumerate(ATTN_GROUPS):
        sl = slice(g * ATTN_HEADS_PER_GROUP, (g + 1) * ATTN_HEADS_PER_GROUP)
        o, lse = dilated_window_attention(q[:, :, sl], k[:, :, sl], v[:, :, sl], window, dilation)
        outs.append(o)
        lses.append(lse)
    alpha = jax.nn.softmax(jnp.stack(lses, axis=0), axis=0)
    y = jnp.einsum('gbth,gbthd->bthd', alpha, jnp.stack(outs, axis=0))
    return y.reshape(B, T, ATTN_OUT_WIDTH).astype(q.dtype)


def centred_depthwise_conv(x, w, b):
    C = x.shape[-1]
    K = w.shape[0]
    left = (K - 1) // 2
    y = lax.conv_general_dilated(x, w[:, None, :].astype(x.dtype), window_strides=(1,), padding=[(left, K - 1 - left)], dimension_numbers=('NWC', 'WIO', 'NWC'), feature_group_count=C)
    return y + b.astype(x.dtype)


def mlstm_chunkwise(q, k, v, log_i, log_f):
    B, H, T, dk = q.shape
    dv = v.shape[-1]
    L = MLSTM_CHUNK
    nc = T // L

    def chunks(a):
        return jnp.moveaxis(a.reshape((B, H, nc, L) + a.shape[3:]), 2, 0)

    causal = jnp.tril(jnp.ones((L, L), dtype=bool))

    def step(carry, inp):
        C, n, m = carry
        qb, kb, vb, ib, fb = inp
        b = jnp.cumsum(fb, axis=-1)
        dmat = jnp.where(causal, b[..., :, None] - b[..., None, :] + ib[..., None, :], NEG_INF)
        inter = b + m[..., None]
        m_t = jnp.maximum(inter, jnp.max(dmat, axis=-1))
        w_inter = jnp.exp(inter - m_t)
        w_intra = jnp.exp(dmat - m_t[..., None])
        qk = jnp.einsum('bhld,bhsd->bhls', qb, kb) * w_intra
        num = w_inter[..., None] * jnp.einsum('bhld,bhde->bhle', qb, C) + jnp.einsum('bhls,bhse->bhle', qk, vb)
        den = w_inter * jnp.einsum('bhld,bhd->bhl', qb, n) + jnp.sum(qk, axis=-1)
        h = num / jnp.maximum(jnp.abs(den), jnp.exp(-m_t))[..., None]
        b_last = b[..., -1]
        g = b_last[..., None] - b + ib
        m_new = jnp.maximum(b_last + m, jnp.max(g, axis=-1))
        decay = jnp.exp(b_last + m - m_new)
        wk = jnp.exp(g - m_new[..., None])
        C_new = decay[..., None, None] * C + jnp.einsum('bhs,bhsd,bhse->bhde', wk, kb, vb)
        n_new = decay[..., None] * n + jnp.einsum('bhs,bhsd->bhd', wk, kb)
        return (C_new, n_new, m_new), h

    init = (jnp.zeros((B, H, dk, dv), jnp.float32), jnp.zeros((B, H, dk), jnp.float32), jnp.zeros((B, H), jnp.float32))
    _, hs = lax.scan(step, init, (chunks(q), chunks(k), chunks(v), chunks(log_i), chunks(log_f)))
    return jnp.moveaxis(hs, 0, 2).reshape(B, H, T, dv)


def mlstm_branch(q, k, v, o, gates, conv_w, conv_b, gate_bias, head_norm_w):
    B, T, _ = q.shape
    H, d = MLSTM_HEADS, MLSTM_HEAD_DIM
    qk = jax.nn.silu(centred_depthwise_conv(jnp.concatenate([q, k], axis=-1), conv_w, conv_b))
    q, k = jnp.split(qk, 2, axis=-1)

    def heads(a):
        return a.reshape(B, T, H, d).transpose(0, 2, 1, 3).astype(jnp.float32)

    qh = heads(q)
    kh = heads(k) * (d ** -0.5)
    vh = heads(v)
    gp = (gates + gate_bias).astype(jnp.float32).reshape(B, T, 4, H).transpose(2, 0, 3, 1)
    i_fwd, f_fwd, i_bwd, f_bwd = gp[0], gp[1], gp[2], gp[3]
    h_fwd = mlstm_chunkwise(qh, kh, vh, i_fwd, jax.nn.log_sigmoid(f_fwd))
    flip = lambda a: jnp.flip(a, axis=2)
    h_bwd = flip(mlstm_chunkwise(flip(qh), flip(kh), flip(vh), jnp.flip(i_bwd, -1), jnp.flip(jax.nn.log_sigmoid(f_bwd), -1)))
    h = (h_fwd + h_bwd).transpose(0, 2, 1, 3)
    mu = jnp.mean(h, axis=-1, keepdims=True)
    var = jnp.mean(jnp.square(h - mu), axis=-1, keepdims=True)
    h = (h - mu) * lax.rsqrt(var + NORM_EPS) * head_norm_w.astype(jnp.float32).reshape(H, d)
    h = jax.nn.sigmoid(o.astype(jnp.float32)) * h.reshape(B, T, MLSTM_WIDTH)
    return h.astype(q.dtype)


def setup_inputs(seed: int = 0) -> dict:
    key = jax.random.key(seed)
    ks = jax.random.split(key, 24)
    f32 = jnp.float32

    def dense(k, fan_in, fan_out):
        return jax.random.normal(k, (DEPTH, fan_in, fan_out), f32) * (fan_in ** -0.5)

    def gain(k, shape):
        return 1.0 + 0.02 * jax.random.normal(k, shape, f32)

    H = MLSTM_HEADS
    i_b = 0.1 * jax.random.normal(ks[9], (DEPTH, 2, H), f32)
    f_b = jnp.linspace(3.0, 6.0, H, dtype=f32)[None, None, :] + 0.1 * jax.random.normal(ks[10], (DEPTH, 2, H), f32)
    gate_bias = jnp.concatenate([i_b[:, 0], f_b[:, 0], i_b[:, 1], f_b[:, 1]], axis=-1)
    return {
        "x": jax.random.normal(ks[0], (BATCH, SEQ, D_MODEL), f32),
        "ffn1_norm": gain(ks[1], (DEPTH, D_MODEL)),
        "ffn1_w_gate": dense(ks[2], D_MODEL, D_FF),
        "ffn1_w_up": dense(ks[3], D_MODEL, D_FF),
        "ffn1_w_down": dense(ks[4], D_FF, D_MODEL),
        "mix_norm": gain(ks[5], (DEPTH, D_MODEL)),
        "w_in": dense(ks[6], D_MODEL, IN_WIDTH),
        "mlstm_conv_w": jax.random.normal(ks[7], (DEPTH, CONV_WIDTH, 2 * MLSTM_WIDTH), f32) * (CONV_WIDTH ** -0.5),
        "mlstm_conv_b": 0.02 * jax.random.normal(ks[8], (DEPTH, 2 * MLSTM_WIDTH), f32),
        "mlstm_gate_bias": gate_bias,
        "mlstm_head_norm": gain(ks[11], (DEPTH, MLSTM_WIDTH)),
        "w_branch_attn": dense(ks[12], ATTN_OUT_WIDTH, D_MODEL),
        "w_branch_mlstm": dense(ks[13], MLSTM_WIDTH, D_MODEL),
        "w_out": dense(ks[14], D_MODEL, D_MODEL),
        "ffn2_norm": gain(ks[15], (DEPTH, D_MODEL)),
        "ffn2_w_gate": dense(ks[16], D_MODEL, D_FF),
        "ffn2_w_up": dense(ks[17], D_MODEL, D_FF),
        "ffn2_w_down": dense(ks[18], D_FF, D_MODEL),
        "final_norm": gain(ks[19], (D_MODEL,)),
    }


def reference(x, ffn1_norm, ffn1_w_gate, ffn1_w_up, ffn1_w_down, mix_norm, w_in, mlstm_conv_w, mlstm_conv_b, mlstm_gate_bias, mlstm_head_norm, w_branch_attn, w_branch_mlstm, w_out, ffn2_norm, ffn2_w_gate, ffn2_w_up, ffn2_w_down, final_norm):
    T = x.shape[1]
    pos = jnp.arange(T)
    for l in range(DEPTH):
        x = x + 0.5 * swiglu(rms_norm(x, ffn1_norm[l]), ffn1_w_gate[l], ffn1_w_up[l], ffn1_w_down[l])
        h = rms_norm(x, mix_norm[l])
        u = h @ w_in[l]
        qa, ka, va, qm, km, vm, om, mgates, g_attn, g_mlstm = split_cols(u, IN_SIZES)
        y_attn = dilated_attention_branch(qa, ka, va, pos)
        y_mlstm = mlstm_branch(qm, km, vm, om, mgates, mlstm_conv_w[l], mlstm_conv_b[l], mlstm_gate_bias[l], mlstm_head_norm[l])
        merged = jax.nn.sigmoid(g_attn) * (y_attn @ w_branch_attn[l]) + jax.nn.sigmoid(g_mlstm) * (y_mlstm @ w_branch_mlstm[l])
        x = x + merged @ w_out[l]
        x = x + 0.5 * swiglu(rms_norm(x, ffn2_norm[l]), ffn2_w_gate[l], ffn2_w_up[l], ffn2_w_down[l])
    return rms_norm(x, final_norm)
```

```python
import functools

import numpy as np
import jax
import jax.numpy as jnp
from jax import lax
from jax.experimental import pallas as pl
from jax.experimental.pallas import tpu as pltpu

F32 = jnp.float32
BF16 = jnp.bfloat16

HEAD_DIM = 128
ATTN_GROUPS = ((128, 1), (512, 4), (2048, 16))
ROPE_DIM = HEAD_DIM // 4
ROPE_THETA = 500000.0
MLSTM_CHUNK = 128
NORM_EPS = 1e-6
NEG_INF = -1e30

LANES = 128
VMEM_LIMIT_BYTES = 56 * 1024 * 1024


def _cparams(n_axes):
    return pltpu.CompilerParams(
        dimension_semantics=("arbitrary",) * n_axes, vmem_limit_bytes=VMEM_LIMIT_BYTES)


def _tile(n, pref):
    if n <= pref:
        return n
    t = (pref // LANES) * LANES
    while t >= LANES:
        if n % t == 0:
            return t
        t -= LANES
    return n


def _sigmoid(x):
    return 1.0 / (1.0 + jnp.exp(-x))


def _rmsnorm_kernel(x_ref, g_ref, o_ref):
    x = x_ref[...]
    ms = jnp.mean(x * x, axis=-1, keepdims=True)
    o_ref[...] = (x * lax.rsqrt(ms + NORM_EPS) * g_ref[...]).astype(o_ref.dtype)


def _rmsnorm(x, g, out_dtype):
    m, d = x.shape
    bm = _tile(m, 256)
    return pl.pallas_call(
        _rmsnorm_kernel,
        grid=(m // bm,),
        in_specs=[pl.BlockSpec((bm, d), lambda i: (i, 0)), pl.BlockSpec((1, d), lambda i: (0, 0))],
        out_specs=pl.BlockSpec((bm, d), lambda i: (i, 0)),
        out_shape=jax.ShapeDtypeStruct((m, d), out_dtype),
        compiler_params=_cparams(1),
        name="rmsnorm",
    )(x, g.reshape(1, d))


def _mm_kernel(*refs, n_lhs, n_rhs, pairs, n_extra, n_out, epilogue, nk):
    lhs = refs[:n_lhs]
    rhs = refs[n_lhs:n_lhs + n_rhs]
    extras = refs[n_lhs + n_rhs:n_lhs + n_rhs + n_extra]
    outs = refs[n_lhs + n_rhs + n_extra:n_lhs + n_rhs + n_extra + n_out]
    accs = refs[n_lhs + n_rhs + n_extra + n_out:]

    def dots():
        return [jnp.dot(lhs[a][...], rhs[b][...], preferred_element_type=F32) for a, b in pairs]

    def finish(vals):
        res = epilogue(vals, [e[...] for e in extras])
        for o, r in zip(outs, res):
            o[...] = r.astype(o.dtype)

    if nk == 1:
        finish(dots())
        return

    k = pl.program_id(2)
    vals = dots()

    @pl.when(k == 0)
    def _():
        for acc, v in zip(accs, vals):
            acc[...] = v

    @pl.when(k > 0)
    def _():
        for acc, v in zip(accs, vals):
            acc[...] += v

    @pl.when(k == nk - 1)
    def _():
        finish([acc[...] for acc in accs])


def _mm(lhs, rhs, pairs, extras, epilogue, out_dtypes, *, bm, bn, bk=None, name):
    m = lhs[0].shape[0]
    n = rhs[0].shape[1]
    bm = _tile(m, bm)
    bn = _tile(n, bn)
    kdim = lhs[0].shape[1]
    bk = kdim if bk is None else _tile(kdim, bk)
    nk = kdim // bk
    if nk > 1:
        assert all(l.shape[1] == kdim for l in lhs)
    in_specs = []
    for l in lhs:
        kb = l.shape[1] if nk == 1 else bk
        in_specs.append(pl.BlockSpec((bm, kb), lambda i, j, k: (i, k)))
    for r in rhs:
        kb = r.shape[0] if nk == 1 else bk
        in_specs.append(pl.BlockSpec((kb, bn), lambda i, j, k: (k, j)))
    extra_arrays = []
    for arr, kind in extras:
        extra_arrays.append(arr)
        if kind == "mn":
            in_specs.append(pl.BlockSpec((bm, bn), lambda i, j, k: (i, j)))
        elif kind == "row":
            in_specs.append(pl.BlockSpec((1, bn), lambda i, j, k: (0, j)))
        else:
            period = arr.shape[0] // bm
            in_specs.append(pl.BlockSpec((bm, LANES), lambda i, j, k, period=period: (i % period, 0)))
    kern = functools.partial(
        _mm_kernel, n_lhs=len(lhs), n_rhs=len(rhs), pairs=tuple(pairs), n_extra=len(extras),
        n_out=len(out_dtypes), epilogue=epilogue, nk=nk)
    outs = pl.pallas_call(
        kern,
        grid=(m // bm, n // bn, nk),
        in_specs=in_specs,
        out_specs=[pl.BlockSpec((bm, bn), lambda i, j, k: (i, j)) for _ in out_dtypes],
        out_shape=[jax.ShapeDtypeStruct((m, n), dt) for dt in out_dtypes],
        scratch_shapes=[pltpu.VMEM((bm, bn), F32) for _ in pairs] if nk > 1 else [],
        compiler_params=_cparams(3),
        name=name,
    )(*lhs, *rhs, *extra_arrays)
    return outs


def _ep_identity(accs, extras):
    return [accs[0]]


def _ep_sigmoid(accs, extras):
    return [_sigmoid(accs[0])]


def _ep_swiglu(accs, extras):
    g, u = accs
    return [g * _sigmoid(g) * u]


def _ep_rotary(accs, extras):
    acc = accs[0]
    c, s1, s2 = extras
    pieces = []
    for h in range(acc.shape[1] // HEAD_DIM):
        x = acc[:, h * HEAD_DIM:(h + 1) * HEAD_DIM]
        half = ROPE_DIM // 2
        pieces.append(x * c + pltpu.roll(x, half, 1) * s1 + pltpu.roll(x, HEAD_DIM - half, 1) * s2)
    return [jnp.concatenate(pieces, axis=1) if len(pieces) > 1 else pieces[0]]


def _ep_merge(accs, extras):
    ya, ym = accs
    ga, gm = extras
    return [ga.astype(F32) * ya + gm.astype(F32) * ym]


def _ep_residual(scale, accs, extras):
    return [extras[0] + scale * accs[0]]


def _attn_kernel(q_ref, kp_ref, kc_ref, kn_ref, vp_ref, vc_ref, vn_ref, o_ref, lse_ref, kx_ref, vx_ref,
                 *, bs, qb, reach, seq, n_heads):
    n = pl.program_id(2)
    kx_ref[0:reach, :] = kp_ref[...]
    kx_ref[reach:reach + bs, :] = kc_ref[...]
    kx_ref[reach + bs:, :] = kn_ref[...]
    vx_ref[0:reach, :] = vp_ref[...]
    vx_ref[reach:reach + bs, :] = vc_ref[...]
    vx_ref[reach + bs:, :] = vn_ref[...]

    kw = qb + 2 * reach
    row = lax.broadcasted_iota(jnp.int32, (qb, kw), 0)
    col = lax.broadcasted_iota(jnp.int32, (qb, kw), 1)
    delta = col - row

    def body(a, carry):
        r0 = pl.multiple_of(a * qb, qb)
        kpos = n * bs + r0 - reach + col
        mask = (delta >= 0) & (delta <= 2 * reach) & (kpos >= 0) & (kpos < seq)
        for h in range(n_heads):
            cs = slice(h * HEAD_DIM, (h + 1) * HEAD_DIM)
            q = q_ref[pl.ds(r0, qb), cs]
            k = kx_ref[pl.ds(r0, kw), cs]
            v = vx_ref[pl.ds(r0, kw), cs]
            s = lax.dot_general(q, k, (((1,), (1,)), ((), ())), preferred_element_type=F32)
            s = jnp.where(mask, s, NEG_INF)
            m = jnp.max(s, axis=1, keepdims=True)
            p = jnp.exp(s - m)
            l = jnp.sum(p, axis=1, keepdims=True)
            o = jnp.dot(p.astype(BF16), v, preferred_element_type=F32)
            o_ref[pl.ds(r0, qb), cs] = o * (1.0 / l)
            lse_ref[pl.ds(r0, qb), cs] = jnp.broadcast_to(m + jnp.log(l), (qb, HEAD_DIM))
        return carry

    lax.fori_loop(0, bs // qb, body, 0)


def _attn_group(q, k, v, batch, seq_len, group, dilation, reach):
    m, wtot = q.shape
    n_groups = len(ATTN_GROUPS)
    w = wtot // n_groups
    n_heads = w // HEAD_DIM
    s_len = seq_len // dilation
    qb = 2 * reach
    bs = min(512, s_len)
    assert s_len % bs == 0 and bs % qb == 0 and seq_len % (dilation * reach) == 0
    hb = bs // reach
    n_hblk = s_len // reach

    def view(a):
        return a.reshape(batch, s_len, dilation * wtot)

    def main_map(b, r, n):
        return (b, n, r * n_groups + group)

    def prev_map(b, r, n):
        return (b, jnp.maximum(n * hb - 1, 0), r * n_groups + group)

    def next_map(b, r, n):
        return (b, jnp.minimum((n + 1) * hb, n_hblk - 1), r * n_groups + group)

    main = pl.BlockSpec((None, bs, w), main_map)
    prev = pl.BlockSpec((None, reach, w), prev_map)
    nxt = pl.BlockSpec((None, reach, w), next_map)
    out_spec = pl.BlockSpec((None, bs, w), lambda b, r, n: (b, n, r))
    kern = functools.partial(_attn_kernel, bs=bs, qb=qb, reach=reach, seq=s_len, n_heads=n_heads)
    o, lse = pl.pallas_call(
        kern,
        grid=(batch, dilation, s_len // bs),
        in_specs=[main, prev, main, nxt, prev, main, nxt],
        out_specs=[out_spec, out_spec],
        out_shape=[jax.ShapeDtypeStruct((batch, s_len, dilation * w), F32)] * 2,
        scratch_shapes=[pltpu.VMEM((bs + 2 * reach, w), BF16), pltpu.VMEM((bs + 2 * reach, w), BF16)],
        compiler_params=_cparams(3),
        name=f"dilated_attn_g{group}",
    )(view(q), view(k), view(k), view(k), view(v), view(v), view(v))
    return o.reshape(m, w), lse.reshape(m, w)


def _attn_mix_kernel(*refs, n_groups):
    o_refs = refs[:n_groups]
    l_refs = refs[n_groups:2 * n_groups]
    y_ref = refs[2 * n_groups]
    lses = [r[...] for r in l_refs]
    mx = functools.reduce(jnp.maximum, lses)
    es = [jnp.exp(l - mx) for l in lses]
    den = functools.reduce(lambda a, b: a + b, es)
    num = functools.reduce(lambda a, b: a + b, [e * r[...] for e, r in zip(es, o_refs)])
    y_ref[...] = (num / den).astype(y_ref.dtype)


def _attn_mix(outs, lses):
    m, w = outs[0].shape
    bm = _tile(m, 256)
    spec = pl.BlockSpec((bm, w), lambda i: (i, 0))
    n_groups = len(outs)
    return pl.pallas_call(
        functools.partial(_attn_mix_kernel, n_groups=n_groups),
        grid=(m // bm,),
        in_specs=[spec] * (2 * n_groups),
        out_specs=spec,
        out_shape=jax.ShapeDtypeStruct((m, w), BF16),
        compiler_params=_cparams(1),
        name="attn_mix",
    )(*outs, *lses)


def _conv_kernel(xp_ref, xc_ref, xn_ref, w_ref, b_ref, s_ref, o_ref, xe_ref, *, bt, n_tblk, ksize, halo):
    t = pl.program_id(1)
    left = (ksize - 1) // 2
    prev = xp_ref[...].astype(F32)
    nxt = xn_ref[...].astype(F32)
    xe_ref[0:halo, :] = jnp.where(t > 0, prev, 0.0)
    xe_ref[halo:halo + bt, :] = xc_ref[...].astype(F32)
    xe_ref[halo + bt:, :] = jnp.where(t < n_tblk - 1, nxt, 0.0)
    acc = xe_ref[pl.ds(halo - left, bt), :] * w_ref[0:1, :]
    for j in range(1, ksize):
        acc = acc + xe_ref[pl.ds(halo - left + j, bt), :] * w_ref[j:j + 1, :]
    y = acc + b_ref[...]
    o_ref[...] = (y * _sigmoid(y) * s_ref[...]).astype(o_ref.dtype)


def _conv_silu(x, w, b, scale, batch, seq_len):
    m, c = x.shape
    ksize = w.shape[0]
    halo = 8
    bt = _tile(seq_len, 512)
    bc = _tile(c, 512)
    n_tblk = seq_len // bt
    hb = bt // halo
    n_hblk = seq_len // halo
    x3 = x.reshape(batch, seq_len, c)
    cur = pl.BlockSpec((None, bt, bc), lambda b_, t, j: (b_, t, j))
    prev = pl.BlockSpec((None, halo, bc), lambda b_, t, j: (b_, jnp.maximum(t * hb - 1, 0), j))
    nxt = pl.BlockSpec((None, halo, bc), lambda b_, t, j: (b_, jnp.minimum((t + 1) * hb, n_hblk - 1), j))
    wspec = pl.BlockSpec((ksize, bc), lambda b_, t, j: (0, j))
    rspec = pl.BlockSpec((1, bc), lambda b_, t, j: (0, j))
    out = pl.pallas_call(
        functools.partial(_conv_kernel, bt=bt, n_tblk=n_tblk, ksize=ksize, halo=halo),
        grid=(batch, n_tblk, c // bc),
        in_specs=[prev, cur, nxt, wspec, rspec, rspec],
        out_specs=cur,
        out_shape=jax.ShapeDtypeStruct((batch, seq_len, c), BF16),
        scratch_shapes=[pltpu.VMEM((bt + 2 * halo, bc), F32)],
        compiler_params=_cparams(3),
        name="mlstm_conv_silu",
    )(x3, x3, x3, w, b.reshape(1, c), scale.reshape(1, c))
    return out.reshape(m, c)


def _gate_prep_kernel(g_ref, b_ref, o_ref, *, n_heads, chunk):
    x = g_ref[...] + b_ref[...]
    h = n_heads
    t = x.shape[1]

    def log_sigmoid(z):
        return jnp.minimum(z, 0.0) - jnp.log(1.0 + jnp.exp(-jnp.abs(z)))

    pos = lax.broadcasted_iota(jnp.int32, (h, t), 1) & (chunk - 1)

    def chunk_cumsum(z, reverse):
        shift = 1
        while shift < chunk:
            if reverse:
                moved = pltpu.roll(z, t - shift, 1)
                keep = pos < chunk - shift
            else:
                moved = pltpu.roll(z, shift, 1)
                keep = pos >= shift
            z = z + jnp.where(keep, moved, 0.0)
            shift *= 2
        return z

    o_ref[0] = chunk_cumsum(log_sigmoid(x[h:2 * h]), False)
    o_ref[1] = x[0:h]
    o_ref[2] = chunk_cumsum(log_sigmoid(x[3 * h:4 * h]), True)
    o_ref[3] = x[2 * h:3 * h]


def _gate_prep(gates_t, bias, n_heads):
    batch, rows, t = gates_t.shape
    return pl.pallas_call(
        functools.partial(_gate_prep_kernel, n_heads=n_heads, chunk=MLSTM_CHUNK),
        grid=(batch,),
        in_specs=[pl.BlockSpec((None, rows, t), lambda b: (b, 0, 0)), pl.BlockSpec((rows, 1), lambda b: (0, 0))],
        out_specs=pl.BlockSpec((None, 4, n_heads, t), lambda b: (b, 0, 0, 0)),
        out_shape=jax.ShapeDtypeStruct((batch, 4, n_heads, t), F32),
        compiler_params=_cparams(1),
        name="mlstm_gate_prep",
    )(gates_t, bias.reshape(rows, 1))


def _mlstm_direction(q_ref, k_ref, v_ref, grow_ref, gcol_ref, h_ref, c_ref, n_ref, m_ref, *, reverse, kind):
    L = q_ref.shape[0]
    q = q_ref[...]
    k = k_ref[...]
    v = v_ref[...]
    b_row = grow_ref[kind:kind + 1, :]
    i_row = grow_ref[kind + 1:kind + 2, :]
    b_col = gcol_ref[:, kind:kind + 1]
    i_col = gcol_ref[:, kind + 1:kind + 2]
    m_prev = m_ref[...]
    c_prev = c_ref[...]
    n_prev = n_ref[...]

    li = lax.broadcasted_iota(jnp.int32, (L, L), 0)
    si = lax.broadcasted_iota(jnp.int32, (L, L), 1)
    visible = (si >= li) if reverse else (si <= li)
    dmat = jnp.where(visible, b_col - b_row + i_row, NEG_INF)
    inter = b_col + m_prev
    m_t = jnp.maximum(inter, jnp.max(dmat, axis=1, keepdims=True))
    w_inter = jnp.exp(inter - m_t)
    w_intra = jnp.exp(dmat - m_t)
    qk = lax.dot_general(q, k, (((1,), (1,)), ((), ())), preferred_element_type=F32) * w_intra
    num = w_inter * jnp.dot(q, c_prev.astype(BF16), preferred_element_type=F32) + jnp.dot(
        qk.astype(BF16), v, preferred_element_type=F32)
    qn = jnp.sum(q.astype(F32) * n_prev, axis=1, keepdims=True)
    den = w_inter * qn + jnp.sum(qk, axis=1, keepdims=True)
    h_ref[...] = num / jnp.maximum(jnp.abs(den), jnp.exp(-m_t))

    b_last = b_col[0:1, :] if reverse else b_col[L - 1:L, :]
    g = b_last - b_col + i_col
    m_new = jnp.maximum(b_last + m_prev, jnp.max(g, axis=0, keepdims=True))
    decay = jnp.exp(b_last + m_prev - m_new)
    kw = k.astype(F32) * jnp.exp(g - m_new)
    c_ref[...] = decay * c_prev + lax.dot_general(
        kw.astype(BF16), v, (((0,), (0,)), ((), ())), preferred_element_type=F32)
    n_ref[...] = decay * n_prev + jnp.sum(kw, axis=0, keepdims=True)
    m_ref[...] = m_new


def _mlstm_kernel(qf, kf, vf, growf, gcolf, qb, kb, vb, growb, gcolb, hf_ref, hb_ref,
                  cf, nf, mf, cb, nb, mb):
    @pl.when(pl.program_id(2) == 0)
    def _():
        for r in (cf, nf, mf, cb, nb, mb):
            r[...] = jnp.zeros(r.shape, r.dtype)

    _mlstm_direction(qf, kf, vf, growf, gcolf, hf_ref, cf, nf, mf, reverse=False, kind=0)
    _mlstm_direction(qb, kb, vb, growb, gcolb, hb_ref, cb, nb, mb, reverse=True, kind=2)


def _mlstm_scan(q, k, v, grow, gcol, batch, seq_len, n_heads):
    m, width = q.shape
    d = width // n_heads
    L = MLSTM_CHUNK
    nc = seq_len // L
    q3, k3, v3 = (a.reshape(batch, seq_len, width) for a in (q, k, v))

    def fwd(b, h, c):
        return (b, c, h)

    def bwd(b, h, c):
        return (b, nc - 1 - c, h)

    def seq_spec(imap):
        return pl.BlockSpec((None, L, d), imap)

    def row_spec(rev):
        return pl.BlockSpec((None, None, 4, L), lambda b, h, c: (b, h, 0, nc - 1 - c if rev else c))

    def col_spec(rev):
        return pl.BlockSpec((None, None, L, 4), lambda b, h, c: (b, h, nc - 1 - c if rev else c, 0))

    hf, hb = pl.pallas_call(
        _mlstm_kernel,
        grid=(batch, n_heads, nc),
        in_specs=[seq_spec(fwd)] * 3 + [row_spec(False), col_spec(False)]
        + [seq_spec(bwd)] * 3 + [row_spec(True), col_spec(True)],
        out_specs=[seq_spec(fwd), seq_spec(bwd)],
        out_shape=[jax.ShapeDtypeStruct((batch, seq_len, width), F32)] * 2,
        scratch_shapes=[pltpu.VMEM((d, d), F32), pltpu.VMEM((1, d), F32), pltpu.VMEM((1, 1), F32)] * 2,
        compiler_params=_cparams(3),
        name="mlstm_scan",
    )(q3, k3, v3, grow, gcol, q3, k3, v3, grow, gcol)
    return hf.reshape(m, width), hb.reshape(m, width)


def _mlstm_post_kernel(hf_ref, hb_ref, o_ref, w_ref, y_ref, *, n_heads):
    d = hf_ref.shape[1] // n_heads
    for h in range(n_heads):
        cs = slice(h * d, (h + 1) * d)
        x = hf_ref[:, cs] + hb_ref[:, cs]
        mu = jnp.mean(x, axis=1, keepdims=True)
        xc = x - mu
        var = jnp.mean(xc * xc, axis=1, keepdims=True)
        y = xc * lax.rsqrt(var + NORM_EPS) * w_ref[:, cs]
        y_ref[:, cs] = (_sigmoid(o_ref[:, cs].astype(F32)) * y).astype(y_ref.dtype)


def _mlstm_post(hf, hb, o, w, n_heads):
    m, width = hf.shape
    bm = _tile(m, 256)
    spec = pl.BlockSpec((bm, width), lambda i: (i, 0))
    return pl.pallas_call(
        functools.partial(_mlstm_post_kernel, n_heads=n_heads),
        grid=(m // bm,),
        in_specs=[spec, spec, spec, pl.BlockSpec((1, width), lambda i: (0, 0))],
        out_specs=spec,
        out_shape=jax.ShapeDtypeStruct((m, width), BF16),
        compiler_params=_cparams(1),
        name="mlstm_post",
    )(hf, hb, o, w.reshape(1, width))


def _rope_tables(seq_len, scale):
    half = ROPE_DIM // 2
    inv_freq = ROPE_THETA ** (-jnp.arange(half, dtype=F32) * 2.0 / ROPE_DIM)
    ang = jnp.arange(seq_len).astype(F32)[:, None] * inv_freq[None, :]
    cos, sin = jnp.cos(ang), jnp.sin(ang)
    zeros = jnp.zeros((seq_len, half), F32)
    rest = HEAD_DIM - ROPE_DIM
    c = jnp.concatenate([cos, cos, jnp.ones((seq_len, rest), F32)], axis=1)
    s1 = jnp.concatenate([zeros, sin, jnp.zeros((seq_len, rest), F32)], axis=1)
    s2 = jnp.concatenate([-sin, zeros, jnp.zeros((seq_len, rest), F32)], axis=1)
    return [(c * scale, "pos"), (s1 * scale, "pos"), (s2 * scale, "pos")]


def _ffn(x, norm_g, w_gate, w_up, w_down):
    d, d_ff = w_gate.shape
    ff_pad = -(-d_ff // 1024) * 1024 if d_ff > 1024 else d_ff
    pad = ff_pad - d_ff
    wg = jnp.pad(w_gate.astype(BF16), ((0, 0), (0, pad)))
    wu = jnp.pad(w_up.astype(BF16), ((0, 0), (0, pad)))
    wd = jnp.pad(w_down.astype(BF16), ((0, pad), (0, 0)))
    h = _rmsnorm(x, norm_g, BF16)
    (act,) = _mm([h], [wg, wu], [(0, 0), (0, 1)], [], _ep_swiglu, [BF16], bm=1024, bn=512, name="ffn_gate_up")
    (y,) = _mm([act], [wd], [(0, 0)], [(x, "mn")], functools.partial(_ep_residual, 0.5), [F32],
               bm=1024, bn=1024, bk=1024, name="ffn_down")
    return y


def _mixers(x, batch, seq_len, mix_norm, w_in, conv_w, conv_b, gate_bias, head_norm, w_battn, w_bmlstm, w_out):
    m, d = x.shape
    attn_out = w_battn.shape[0]
    attn_w = len(ATTN_GROUPS) * attn_out
    mw = w_bmlstm.shape[0]
    n_mheads = gate_bias.shape[0] // 4
    mhd = mw // n_mheads
    offs = np.cumsum([0, attn_w, attn_w, attn_w, mw, mw, mw, mw, 4 * n_mheads, d, d]).tolist()
    wcols = lambda a, b: w_in[:, offs[a]:offs[b]].astype(BF16)

    h = _rmsnorm(x, mix_norm, BF16)
    (qa,) = _mm([h], [wcols(0, 1)], [(0, 0)], _rope_tables(seq_len, HEAD_DIM ** -0.5), _ep_rotary, [BF16],
                bm=1024, bn=1024, name="proj_attn_q")
    (ka,) = _mm([h], [wcols(1, 2)], [(0, 0)], _rope_tables(seq_len, 1.0), _ep_rotary, [BF16],
                bm=1024, bn=1024, name="proj_attn_k")
    (va,) = _mm([h], [wcols(2, 3)], [(0, 0)], [], _ep_identity, [BF16], bm=1024, bn=1024, name="proj_attn_v")
    (qkm,) = _mm([h], [wcols(3, 5)], [(0, 0)], [], _ep_identity, [BF16], bm=1024, bn=1024, name="proj_mlstm_qk")
    (vom,) = _mm([h], [wcols(5, 7)], [(0, 0)], [], _ep_identity, [BF16], bm=1024, bn=1024, name="proj_mlstm_vo")
    n_gate = 4 * n_mheads
    wg = jnp.pad(wcols(7, 8), ((0, 0), (0, LANES - n_gate)))
    (gates,) = _mm([h], [wg], [(0, 0)], [], _ep_identity, [F32], bm=1024, bn=LANES, name="proj_mlstm_gates")
    (bgate,) = _mm([h], [wcols(8, 10)], [(0, 0)], [], _ep_sigmoid, [BF16], bm=1024, bn=1024, name="proj_branch_gates")

    outs, lses = [], []
    for g, (window, dilation) in enumerate(ATTN_GROUPS):
        o, lse = _attn_group(qa, ka, va, batch, seq_len, g, dilation, window // (2 * dilation))
        outs.append(o)
        lses.append(lse)
    y_attn = _attn_mix(outs, lses)

    k_scale = jnp.concatenate([jnp.ones((mw,), F32), jnp.full((mw,), mhd ** -0.5, F32)])
    qk = _conv_silu(qkm, conv_w, conv_b, k_scale, batch, seq_len)
    gates_t = jnp.swapaxes(gates[:, :n_gate].reshape(batch, seq_len, n_gate), 1, 2)
    gp = _gate_prep(gates_t, gate_bias, n_mheads)
    grow = jnp.transpose(gp, (0, 2, 1, 3))
    gcol = jnp.transpose(gp, (0, 2, 3, 1))
    hf, hb = _mlstm_scan(qk[:, :mw], qk[:, mw:], vom[:, :mw], grow, gcol, batch, seq_len, n_mheads)
    y_mlstm = _mlstm_post(hf, hb, vom[:, mw:], head_norm, n_mheads)

    (merged,) = _mm([y_attn, y_mlstm], [w_battn.astype(BF16), w_bmlstm.astype(BF16)], [(0, 0), (1, 1)],
                    [(bgate[:, :d], "mn"), (bgate[:, d:], "mn")], _ep_merge, [BF16],
                    bm=1024, bn=1024, name="branch_merge")
    (y,) = _mm([merged], [w_out.astype(BF16)], [(0, 0)], [(x, "mn")], functools.partial(_ep_residual, 1.0), [F32],
               bm=1024, bn=1024, name="out_proj")
    return y


def kernel(x, ffn1_norm, ffn1_w_gate, ffn1_w_up, ffn1_w_down, mix_norm, w_in, mlstm_conv_w, mlstm_conv_b, mlstm_gate_bias, mlstm_head_norm, w_branch_attn, w_branch_mlstm, w_out, ffn2_norm, ffn2_w_gate, ffn2_w_up, ffn2_w_down, final_norm):
    batch, seq_len, d = x.shape
    h = x.reshape(batch * seq_len, d)
    for l in range(ffn1_norm.shape[0]):
        h = _ffn(h, ffn1_norm[l], ffn1_w_gate[l], ffn1_w_up[l], ffn1_w_down[l])
        h = _mixers(h, batch, seq_len, mix_norm[l], w_in[l], mlstm_conv_w[l], mlstm_conv_b[l], mlstm_gate_bias[l],
                    mlstm_head_norm[l], w_branch_attn[l], w_branch_mlstm[l], w_out[l])
        h = _ffn(h, ffn2_norm[l], ffn2_w_gate[l], ffn2_w_up[l], ffn2_w_down[l])
    return _rmsnorm(h, final_norm, F32).reshape(batch, seq_len, d)
```

```python
import functools

import numpy as np
import jax
import jax.numpy as jnp
from jax import lax
from jax.experimental import pallas as pl
from jax.experimental.pallas import tpu as pltpu

F32 = jnp.float32
BF16 = jnp.bfloat16

HEAD_DIM = 128
ATTN_GROUPS = ((128, 1), (512, 4), (2048, 16))
ROPE_DIM = HEAD_DIM // 4
ROPE_THETA = 500000.0
MLSTM_CHUNK = 128
NORM_EPS = 1e-6
NEG_INF = -1e30

LANES = 128
VMEM_LIMIT_BYTES = 56 * 1024 * 1024


def _cparams(n_axes):
    return pltpu.CompilerParams(
        dimension_semantics=("arbitrary",) * n_axes, vmem_limit_bytes=VMEM_LIMIT_BYTES)


def _tile(n, pref):
    if n <= pref:
        return n
    t = (pref // LANES) * LANES
    while t >= LANES:
        if n % t == 0:
            return t
        t -= LANES
    return n


def _sigmoid(x):
    return 1.0 / (1.0 + jnp.exp(-x))


def _rmsnorm_kernel(x_ref, g_ref, o_ref):
    x = x_ref[...]
    ms = jnp.mean(x * x, axis=-1, keepdims=True)
    o_ref[...] = (x * lax.rsqrt(ms + NORM_EPS) * g_ref[...]).astype(o_ref.dtype)


def _rmsnorm(x, g, out_dtype):
    m, d = x.shape
    bm = _tile(m, 256)
    return pl.pallas_call(
        _rmsnorm_kernel,
        grid=(m // bm,),
        in_specs=[pl.BlockSpec((bm, d), lambda i: (i, 0)), pl.BlockSpec((1, d), lambda i: (0, 0))],
        out_specs=pl.BlockSpec((bm, d), lambda i: (i, 0)),
        out_shape=jax.ShapeDtypeStruct((m, d), out_dtype),
        compiler_params=_cparams(1),
        name="rmsnorm",
    )(x, g.reshape(1, d))


def _mm_kernel(*refs, n_lhs, n_rhs, pairs, n_extra, n_out, epilogue, nk):
    lhs = refs[:n_lhs]
    rhs = refs[n_lhs:n_lhs + n_rhs]
    extras = refs[n_lhs + n_rhs:n_lhs + n_rhs + n_extra]
    outs = refs[n_lhs + n_rhs + n_extra:n_lhs + n_rhs + n_extra + n_out]
    accs = refs[n_lhs + n_rhs + n_extra + n_out:]

    def dots():
        return [jnp.dot(lhs[a][...], rhs[b][...], preferred_element_type=F32) for a, b in pairs]

    def finish(vals):
        res = epilogue(vals, [e[...] for e in extras])
        for o, r in zip(outs, res):
            o[...] = r.astype(o.dtype)

    if nk == 1:
        finish(dots())
        return

    k = pl.program_id(2)
    vals = dots()

    @pl.when(k == 0)
    def _():
        for acc, v in zip(accs, vals):
            acc[...] = v

    @pl.when(k > 0)
    def _():
        for acc, v in zip(accs, vals):
            acc[...] += v

    @pl.when(k == nk - 1)
    def _():
        finish([acc[...] for acc in accs])


def _mm(lhs, rhs, pairs, extras, epilogue, out_dtypes, *, bm, bn, bk=None, name):
    m = lhs[0].shape[0]
    n = rhs[0].shape[1]
    bm = _tile(m, bm)
    bn = _tile(n, bn)
    kdim = lhs[0].shape[1]
    bk = kdim if bk is None else _tile(kdim, bk)
    nk = kdim // bk
    if nk > 1:
        assert all(l.shape[1] == kdim for l in lhs)
    in_specs = []
    for l in lhs:
        kb = l.shape[1] if nk == 1 else bk
        in_specs.append(pl.BlockSpec((bm, kb), lambda i, j, k: (i, k)))
    for r in rhs:
        kb = r.shape[0] if nk == 1 else bk
        in_specs.append(pl.BlockSpec((kb, bn), lambda i, j, k: (k, j)))
    extra_arrays = []
    for arr, kind in extras:
        extra_arrays.append(arr)
        if kind == "mn" or isinstance(kind, tuple):
            off = 0 if kind == "mn" else kind[1] // bn
            in_specs.append(pl.BlockSpec((bm, bn), lambda i, j, k, off=off: (i, j + off)))
        elif kind == "row":
            in_specs.append(pl.BlockSpec((1, bn), lambda i, j, k: (0, j)))
        else:
            period = arr.shape[0] // bm
            in_specs.append(pl.BlockSpec((bm, LANES), lambda i, j, k, period=period: (i % period, 0)))
    kern = functools.partial(
        _mm_kernel, n_lhs=len(lhs), n_rhs=len(rhs), pairs=tuple(pairs), n_extra=len(extras),
        n_out=len(out_dtypes), epilogue=epilogue, nk=nk)
    outs = pl.pallas_call(
        kern,
        grid=(m // bm, n // bn, nk),
        in_specs=in_specs,
        out_specs=[pl.BlockSpec((bm, bn), lambda i, j, k: (i, j)) for _ in out_dtypes],
        out_shape=[jax.ShapeDtypeStruct((m, n), dt) for dt in out_dtypes],
        scratch_shapes=[pltpu.VMEM((bm, bn), F32) for _ in pairs] if nk > 1 else [],
        compiler_params=_cparams(3),
        name=name,
    )(*lhs, *rhs, *extra_arrays)
    return outs


def _ep_identity(accs, extras):
    return [accs[0]]


def _ep_sigmoid(accs, extras):
    return [_sigmoid(accs[0])]


def _ep_swiglu(accs, extras):
    g, u = accs
    return [g * _sigmoid(g) * u]


def _ep_rotary(accs, extras):
    acc = accs[0]
    c, s1, s2 = extras
    pieces = []
    for h in range(acc.shape[1] // HEAD_DIM):
        x = acc[:, h * HEAD_DIM:(h + 1) * HEAD_DIM]
        half = ROPE_DIM // 2
        pieces.append(x * c + pltpu.roll(x, half, 1) * s1 + pltpu.roll(x, HEAD_DIM - half, 1) * s2)
    return [jnp.concatenate(pieces, axis=1) if len(pieces) > 1 else pieces[0]]


def _ep_merge(accs, extras):
    ya, ym = accs
    ga, gm = extras
    return [ga.astype(F32) * ya + gm.astype(F32) * ym]


def _ep_residual(scale, accs, extras):
    return [extras[0] + scale * accs[0]]


def _attn_kernel(q_ref, kp_ref, kc_ref, kn_ref, vp_ref, vc_ref, vn_ref, o_ref, lse_ref, kx_ref, vx_ref,
                 *, bs, qb, reach, seq, n_heads):
    n = pl.program_id(2)
    kx_ref[0:reach, :] = kp_ref[...]
    kx_ref[reach:reach + bs, :] = kc_ref[...]
    kx_ref[reach + bs:, :] = kn_ref[...]
    vx_ref[0:reach, :] = vp_ref[...]
    vx_ref[reach:reach + bs, :] = vc_ref[...]
    vx_ref[reach + bs:, :] = vn_ref[...]

    kw = qb + 2 * reach
    row = lax.broadcasted_iota(jnp.int32, (qb, kw), 0)
    col = lax.broadcasted_iota(jnp.int32, (qb, kw), 1)
    delta = col - row

    def body(a, carry):
        r0 = pl.multiple_of(a * qb, qb)
        kpos = n * bs + r0 - reach + col
        mask = (delta >= 0) & (delta <= 2 * reach) & (kpos >= 0) & (kpos < seq)
        for h in range(n_heads):
            cs = slice(h * HEAD_DIM, (h + 1) * HEAD_DIM)
            q = q_ref[pl.ds(r0, qb), cs]
            k = kx_ref[pl.ds(r0, kw), cs]
            v = vx_ref[pl.ds(r0, kw), cs]
            s = lax.dot_general(q, k, (((1,), (1,)), ((), ())), preferred_element_type=F32)
            s = jnp.where(mask, s, NEG_INF)
            m = jnp.max(s, axis=1, keepdims=True)
            p = jnp.exp(s - m)
            l = jnp.sum(p, axis=1, keepdims=True)
            o = jnp.dot(p.astype(BF16), v, preferred_element_type=F32)
            o_ref[h, pl.ds(r0, qb), :] = o * (1.0 / l)
            lse_ref[h, pl.ds(r0, qb), :] = jnp.broadcast_to(m + jnp.log(l), (qb, HEAD_DIM))
        return carry

    lax.fori_loop(0, bs // qb, body, 0)


def _attn_proj_kernel(*refs, rotary, dilation):
    if rotary:
        h_ref, w_ref, c_ref, s1_ref, s2_ref, o_ref = refs[:6]
    else:
        h_ref, w_ref, o_ref = refs[:3]
    acc = jnp.dot(h_ref[...], w_ref[...], preferred_element_type=F32)
    if rotary:
        (acc,) = _ep_rotary([acc], [c_ref[...], s1_ref[...], s2_ref[...]])
    if dilation == 1:
        o_ref[...] = acc.astype(o_ref.dtype)
    else:
        scr = refs[-1]
        rows = acc.shape[0] // dilation
        for c in range(acc.shape[1] // LANES):
            scr[c] = acc[:, c * LANES:(c + 1) * LANES]
        for r in range(dilation):
            for c in range(acc.shape[1] // LANES):
                o_ref[r, :, c * LANES:(c + 1) * LANES] = scr[c, pl.ds(r, rows, stride=dilation), :].astype(o_ref.dtype)


def _attn_proj(h, w, group, tables, batch, seq_len, dilation, name):
    m, d_model = h.shape
    wg = w.shape[1] // len(ATTN_GROUPS)
    bm = _tile(seq_len, 1024)
    nbt = seq_len // bm
    assert bm % (dilation * 16) == 0
    in_specs = [pl.BlockSpec((bm, d_model), lambda i: (i, 0)), pl.BlockSpec((d_model, wg), lambda i: (0, group))]
    args = [h, w]
    for tab in tables:
        in_specs.append(pl.BlockSpec((bm, LANES), lambda i: (i % nbt, 0)))
        args.append(tab)
    if dilation == 1:
        out_spec = pl.BlockSpec((None, None, bm, wg), lambda i: (i // nbt, 0, i % nbt, 0))
        scratch = []
    else:
        out_spec = pl.BlockSpec((None, dilation, bm // dilation, wg), lambda i: (i // nbt, 0, i % nbt, 0))
        scratch = [pltpu.VMEM((wg // LANES, bm, LANES), F32)]
    return pl.pallas_call(
        functools.partial(_attn_proj_kernel, rotary=bool(tables), dilation=dilation),
        grid=(m // bm,),
        in_specs=in_specs,
        out_specs=out_spec,
        out_shape=jax.ShapeDtypeStruct((batch, dilation, seq_len // dilation, wg), BF16),
        scratch_shapes=scratch,
        compiler_params=_cparams(1),
        name=name,
    )(*args)


def _attn_group(q, k, v, group, reach):
    batch, dilation, s_len, w = q.shape
    n_heads = w // HEAD_DIM
    qb = 2 * reach
    bs = min(512, s_len)
    assert s_len % bs == 0 and bs % qb == 0
    hb = bs // reach
    n_hblk = s_len // reach

    main = pl.BlockSpec((None, None, bs, w), lambda b, r, n: (b, r, n, 0))
    prev = pl.BlockSpec((None, None, reach, w), lambda b, r, n: (b, r, jnp.maximum(n * hb - 1, 0), 0))
    nxt = pl.BlockSpec((None, None, reach, w), lambda b, r, n: (b, r, jnp.minimum((n + 1) * hb, n_hblk - 1), 0))
    out_spec = pl.BlockSpec((None, None, n_heads, bs, HEAD_DIM), lambda b, r, n: (b, r, 0, n, 0))
    kern = functools.partial(_attn_kernel, bs=bs, qb=qb, reach=reach, seq=s_len, n_heads=n_heads)
    return pl.pallas_call(
        kern,
        grid=(batch, dilation, s_len // bs),
        in_specs=[main, prev, main, nxt, prev, main, nxt],
        out_specs=[out_spec, out_spec],
        out_shape=[jax.ShapeDtypeStruct((batch, dilation, n_heads, s_len, HEAD_DIM), F32)] * 2,
        scratch_shapes=[pltpu.VMEM((bs + 2 * reach, w), BF16), pltpu.VMEM((bs + 2 * reach, w), BF16)],
        compiler_params=_cparams(3),
        name=f"dilated_attn_g{group}",
    )(q, k, k, k, v, v, v)


def _attn_mix_kernel(*refs, dilations, bt):
    n_groups = len(dilations)
    o_refs = refs[:n_groups]
    l_refs = refs[n_groups:2 * n_groups]
    y_ref = refs[2 * n_groups]
    scr = refs[2 * n_groups + 1]
    n_heads = scr.shape[0]
    dmax = max(dilations)
    rows = bt // dmax
    for r in range(dmax):
        for h in range(n_heads):
            def rd(ref, d):
                return ref[r % d, h, pl.ds(r // d, rows, stride=dmax // d), :]
            lses = [rd(ref, d) for ref, d in zip(l_refs, dilations)]
            mx = functools.reduce(jnp.maximum, lses)
            es = [jnp.exp(l - mx) for l in lses]
            den = functools.reduce(lambda a, b: a + b, es)
            num = functools.reduce(lambda a, b: a + b, [e * rd(ref, d) for e, ref, d in zip(es, o_refs, dilations)])
            scr[h, pl.ds(r, rows, stride=dmax), :] = num / den
    for h in range(n_heads):
        y_ref[:, h * HEAD_DIM:(h + 1) * HEAD_DIM] = scr[h].astype(y_ref.dtype)


def _attn_mix(outs, lses, dilations, seq_len):
    batch, _, n_heads, _, hd = outs[0].shape
    dmax = max(dilations)
    bt = _tile(seq_len, 512)
    assert bt % (8 * dmax) == 0 and all(dmax % d == 0 for d in dilations)
    nbt = seq_len // bt
    specs = [pl.BlockSpec((None, d, n_heads, bt // d, hd), lambda b, i: (b, 0, 0, i, 0)) for d in dilations]
    return pl.pallas_call(
        functools.partial(_attn_mix_kernel, dilations=tuple(dilations), bt=bt),
        grid=(batch, nbt),
        in_specs=specs + specs,
        out_specs=pl.BlockSpec((bt, n_heads * hd), lambda b, i: (b * nbt + i, 0)),
        out_shape=jax.ShapeDtypeStruct((batch * seq_len, n_heads * hd), BF16),
        scratch_shapes=[pltpu.VMEM((n_heads, bt, hd), F32)],
        compiler_params=_cparams(2),
        name="attn_mix",
    )(*outs, *lses)


def _conv_kernel(xp_ref, xc_ref, xn_ref, w_ref, b_ref, s_ref, o_ref, xe_ref, *, bt, n_tblk, ksize, halo):
    t = pl.program_id(1)
    left = (ksize - 1) // 2
    prev = xp_ref[...].astype(F32)
    nxt = xn_ref[...].astype(F32)
    xe_ref[0:halo, :] = jnp.where(t > 0, prev, 0.0)
    xe_ref[halo:halo + bt, :] = xc_ref[...].astype(F32)
    xe_ref[halo + bt:, :] = jnp.where(t < n_tblk - 1, nxt, 0.0)
    acc = xe_ref[pl.ds(halo - left, bt), :] * w_ref[0:1, :]
    for j in range(1, ksize):
        acc = acc + xe_ref[pl.ds(halo - left + j, bt), :] * w_ref[j:j + 1, :]
    y = acc + b_ref[...]
    o_ref[...] = (y * _sigmoid(y) * s_ref[...]).astype(o_ref.dtype)


def _conv_silu(x, w, b, scale, batch, seq_len):
    m, c = x.shape
    ksize = w.shape[0]
    halo = 8
    bt = _tile(seq_len, 512)
    bc = _tile(c, 512)
    n_tblk = seq_len // bt
    hb = bt // halo
    n_hblk = seq_len // halo
    x3 = x.reshape(batch, seq_len, c)
    cur = pl.BlockSpec((None, bt, bc), lambda b_, t, j: (b_, t, j))
    prev = pl.BlockSpec((None, halo, bc), lambda b_, t, j: (b_, jnp.maximum(t * hb - 1, 0), j))
    nxt = pl.BlockSpec((None, halo, bc), lambda b_, t, j: (b_, jnp.minimum((t + 1) * hb, n_hblk - 1), j))
    wspec = pl.BlockSpec((ksize, bc), lambda b_, t, j: (0, j))
    rspec = pl.BlockSpec((1, bc), lambda b_, t, j: (0, j))
    out = pl.pallas_call(
        functools.partial(_conv_kernel, bt=bt, n_tblk=n_tblk, ksize=ksize, halo=halo),
        grid=(batch, n_tblk, c // bc),
        in_specs=[prev, cur, nxt, wspec, rspec, rspec],
        out_specs=cur,
        out_shape=jax.ShapeDtypeStruct((batch, seq_len, c), BF16),
        scratch_shapes=[pltpu.VMEM((bt + 2 * halo, bc), F32)],
        compiler_params=_cparams(3),
        name="mlstm_conv_silu",
    )(x3, x3, x3, w, b.reshape(1, c), scale.reshape(1, c))
    return out.reshape(m, c)


def _gate_prep_kernel(g_ref, b_ref, o_ref, *, n_heads, chunk):
    x = g_ref[...] + b_ref[...]
    h = n_heads
    t = x.shape[1]

    def log_sigmoid(z):
        return jnp.minimum(z, 0.0) - jnp.log(1.0 + jnp.exp(-jnp.abs(z)))

    pos = lax.broadcasted_iota(jnp.int32, (h, t), 1) & (chunk - 1)

    def chunk_cumsum(z, reverse):
        shift = 1
        while shift < chunk:
            if reverse:
                moved = pltpu.roll(z, t - shift, 1)
                keep = pos < chunk - shift
            else:
                moved = pltpu.roll(z, shift, 1)
                keep = pos >= shift
            z = z + jnp.where(keep, moved, 0.0)
            shift *= 2
        return z

    o_ref[0] = chunk_cumsum(log_sigmoid(x[h:2 * h]), False)
    o_ref[1] = x[0:h]
    o_ref[2] = chunk_cumsum(log_sigmoid(x[3 * h:4 * h]), True)
    o_ref[3] = x[2 * h:3 * h]


def _gate_prep(gates_t, bias, n_heads):
    batch, rows, t = gates_t.shape
    return pl.pallas_call(
        functools.partial(_gate_prep_kernel, n_heads=n_heads, chunk=MLSTM_CHUNK),
        grid=(batch,),
        in_specs=[pl.BlockSpec((None, rows, t), lambda b: (b, 0, 0)), pl.BlockSpec((rows, 1), lambda b: (0, 0))],
        out_specs=pl.BlockSpec((None, 4, n_heads, t), lambda b: (b, 0, 0, 0)),
        out_shape=jax.ShapeDtypeStruct((batch, 4, n_heads, t), F32),
        compiler_params=_cparams(1),
        name="mlstm_gate_prep",
    )(gates_t, bias.reshape(rows, 1))


def _mlstm_direction(q_ref, k_ref, v_ref, grow_ref, gcol_ref, h_ref, c_ref, n_ref, m_ref, *, reverse, kind):
    L = q_ref.shape[0]
    q = q_ref[...]
    k = k_ref[...]
    v = v_ref[...]
    b_row = grow_ref[kind:kind + 1, :]
    i_row = grow_ref[kind + 1:kind + 2, :]
    b_col = gcol_ref[:, kind:kind + 1]
    i_col = gcol_ref[:, kind + 1:kind + 2]
    m_prev = m_ref[...]
    c_prev = c_ref[...]
    n_prev = n_ref[...]

    li = lax.broadcasted_iota(jnp.int32, (L, L), 0)
    si = lax.broadcasted_iota(jnp.int32, (L, L), 1)
    visible = (si >= li) if reverse else (si <= li)
    dmat = jnp.where(visible, b_col - b_row + i_row, NEG_INF)
    inter = b_col + m_prev
    m_t = jnp.maximum(inter, jnp.max(dmat, axis=1, keepdims=True))
    w_inter = jnp.exp(inter - m_t)
    w_intra = jnp.exp(dmat - m_t)
    qk = lax.dot_general(q, k, (((1,), (1,)), ((), ())), preferred_element_type=F32) * w_intra
    num = w_inter * jnp.dot(q, c_prev.astype(BF16), preferred_element_type=F32) + jnp.dot(
        qk.astype(BF16), v, preferred_element_type=F32)
    qn = jnp.sum(q.astype(F32) * n_prev, axis=1, keepdims=True)
    den = w_inter * qn + jnp.sum(qk, axis=1, keepdims=True)
    h_ref[...] = num / jnp.maximum(jnp.abs(den), jnp.exp(-m_t))

    b_last = b_col[0:1, :] if reverse else b_col[L - 1:L, :]
    g = b_last - b_col + i_col
    m_new = jnp.maximum(b_last + m_prev, jnp.max(g, axis=0, keepdims=True))
    decay = jnp.exp(b_last + m_prev - m_new)
    kw = k.astype(F32) * jnp.exp(g - m_new)
    c_ref[...] = decay * c_prev + lax.dot_general(
        kw.astype(BF16), v, (((0,), (0,)), ((), ())), preferred_element_type=F32)
    n_ref[...] = decay * n_prev + jnp.sum(kw, axis=0, keepdims=True)
    m_ref[...] = m_new


def _mlstm_kernel(qf, kf, vf, growf, gcolf, qb, kb, vb, growb, gcolb, hf_ref, hb_ref,
                  cf, nf, mf, cb, nb, mb):
    @pl.when(pl.program_id(2) == 0)
    def _():
        for r in (cf, nf, mf, cb, nb, mb):
            r[...] = jnp.zeros(r.shape, r.dtype)

    _mlstm_direction(qf, kf, vf, growf, gcolf, hf_ref, cf, nf, mf, reverse=False, kind=0)
    _mlstm_direction(qb, kb, vb, growb, gcolb, hb_ref, cb, nb, mb, reverse=True, kind=2)


def _mlstm_scan(qk, vo, grow, gcol, batch, seq_len, n_heads):
    m, width2 = qk.shape
    width = width2 // 2
    d = width // n_heads
    L = MLSTM_CHUNK
    nc = seq_len // L
    qk3 = qk.reshape(batch, seq_len, width2)
    vo3 = vo.reshape(batch, seq_len, width2)

    def seq_spec(rev, col0):
        return pl.BlockSpec((None, L, d), lambda b, h, c: (b, nc - 1 - c if rev else c, h + col0))

    def row_spec(rev):
        return pl.BlockSpec((None, None, 4, L), lambda b, h, c: (b, h, 0, nc - 1 - c if rev else c))

    def col_spec(rev):
        return pl.BlockSpec((None, None, L, 4), lambda b, h, c: (b, h, nc - 1 - c if rev else c, 0))

    def direction(rev):
        return [seq_spec(rev, 0), seq_spec(rev, n_heads), seq_spec(rev, 0), row_spec(rev), col_spec(rev)]

    hf, hb = pl.pallas_call(
        _mlstm_kernel,
        grid=(batch, n_heads, nc),
        in_specs=direction(False) + direction(True),
        out_specs=[seq_spec(False, 0), seq_spec(True, 0)],
        out_shape=[jax.ShapeDtypeStruct((batch, seq_len, width), F32)] * 2,
        scratch_shapes=[pltpu.VMEM((d, d), F32), pltpu.VMEM((1, d), F32), pltpu.VMEM((1, 1), F32)] * 2,
        compiler_params=_cparams(3),
        name="mlstm_scan",
    )(qk3, qk3, vo3, grow, gcol, qk3, qk3, vo3, grow, gcol)
    return hf.reshape(m, width), hb.reshape(m, width)


def _mlstm_post_kernel(hf_ref, hb_ref, o_ref, w_ref, y_ref, *, n_heads):
    d = hf_ref.shape[1] // n_heads
    for h in range(n_heads):
        cs = slice(h * d, (h + 1) * d)
        x = hf_ref[:, cs] + hb_ref[:, cs]
        mu = jnp.mean(x, axis=1, keepdims=True)
        xc = x - mu
        var = jnp.mean(xc * xc, axis=1, keepdims=True)
        y = xc * lax.rsqrt(var + NORM_EPS) * w_ref[:, cs]
        y_ref[:, cs] = (_sigmoid(o_ref[:, cs].astype(F32)) * y).astype(y_ref.dtype)


def _mlstm_post(hf, hb, vo, w, n_heads):
    m, width = hf.shape
    bm = _tile(m, 256)
    spec = pl.BlockSpec((bm, width), lambda i: (i, 0))
    return pl.pallas_call(
        functools.partial(_mlstm_post_kernel, n_heads=n_heads),
        grid=(m // bm,),
        in_specs=[spec, spec, pl.BlockSpec((bm, width), lambda i: (i, 1)), pl.BlockSpec((1, width), lambda i: (0, 0))],
        out_specs=spec,
        out_shape=jax.ShapeDtypeStruct((m, width), BF16),
        compiler_params=_cparams(1),
        name="mlstm_post",
    )(hf, hb, vo, w.reshape(1, width))


def _rope_tables(seq_len, scale):
    half = ROPE_DIM // 2
    inv_freq = ROPE_THETA ** (-jnp.arange(half, dtype=F32) * 2.0 / ROPE_DIM)
    ang = jnp.arange(seq_len).astype(F32)[:, None] * inv_freq[None, :]
    cos, sin = jnp.cos(ang), jnp.sin(ang)
    zeros = jnp.zeros((seq_len, half), F32)
    rest = HEAD_DIM - ROPE_DIM
    c = jnp.concatenate([cos, cos, jnp.ones((seq_len, rest), F32)], axis=1)
    s1 = jnp.concatenate([zeros, sin, jnp.zeros((seq_len, rest), F32)], axis=1)
    s2 = jnp.concatenate([-sin, zeros, jnp.zeros((seq_len, rest), F32)], axis=1)
    return [(c * scale, "pos"), (s1 * scale, "pos"), (s2 * scale, "pos")]


def _ffn(x, norm_g, w_gate, w_up, w_down):
    d, d_ff = w_gate.shape
    ff_pad = -(-d_ff // 1024) * 1024 if d_ff > 1024 else d_ff
    pad = ff_pad - d_ff
    wg = jnp.pad(w_gate.astype(BF16), ((0, 0), (0, pad)))
    wu = jnp.pad(w_up.astype(BF16), ((0, 0), (0, pad)))
    wd = jnp.pad(w_down.astype(BF16), ((0, pad), (0, 0)))
    h = _rmsnorm(x, norm_g, BF16)
    (act,) = _mm([h], [wg, wu], [(0, 0), (0, 1)], [], _ep_swiglu, [BF16], bm=1024, bn=512, name="ffn_gate_up")
    (y,) = _mm([act], [wd], [(0, 0)], [(x, "mn")], functools.partial(_ep_residual, 0.5), [F32],
               bm=512, bn=512, name="ffn_down")
    return y


def _mixers(x, batch, seq_len, mix_norm, w_in, conv_w, conv_b, gate_bias, head_norm, w_battn, w_bmlstm, w_out):
    m, d = x.shape
    attn_out = w_battn.shape[0]
    attn_w = len(ATTN_GROUPS) * attn_out
    mw = w_bmlstm.shape[0]
    n_mheads = gate_bias.shape[0] // 4
    mhd = mw // n_mheads
    offs = np.cumsum([0, attn_w, attn_w, attn_w, mw, mw, mw, mw, 4 * n_mheads, d, d]).tolist()
    wcols = lambda a, b: w_in[:, offs[a]:offs[b]].astype(BF16)

    h = _rmsnorm(x, mix_norm, BF16)
    (qkm,) = _mm([h], [wcols(3, 5)], [(0, 0)], [], _ep_identity, [BF16], bm=1024, bn=1024, name="proj_mlstm_qk")
    (vom,) = _mm([h], [wcols(5, 7)], [(0, 0)], [], _ep_identity, [BF16], bm=1024, bn=1024, name="proj_mlstm_vo")
    n_gate = 4 * n_mheads
    wg = jnp.pad(wcols(7, 8), ((0, 0), (0, LANES - n_gate)))
    (gates,) = _mm([h], [wg], [(0, 0)], [], _ep_identity, [F32], bm=1024, bn=LANES, name="proj_mlstm_gates")
    (bgate,) = _mm([h], [wcols(8, 10)], [(0, 0)], [], _ep_sigmoid, [BF16], bm=1024, bn=1024, name="proj_branch_gates")

    wq, wk, wv = wcols(0, 1), wcols(1, 2), wcols(2, 3)
    q_tabs = [t for t, _ in _rope_tables(seq_len, HEAD_DIM ** -0.5)]
    k_tabs = [t for t, _ in _rope_tables(seq_len, 1.0)]
    outs, lses, dilations = [], [], []
    for g, (window, dilation) in enumerate(ATTN_GROUPS):
        qg = _attn_proj(h, wq, g, q_tabs, batch, seq_len, dilation, f"proj_attn_q{g}")
        kg = _attn_proj(h, wk, g, k_tabs, batch, seq_len, dilation, f"proj_attn_k{g}")
        vg = _attn_proj(h, wv, g, [], batch, seq_len, dilation, f"proj_attn_v{g}")
        o, lse = _attn_group(qg, kg, vg, g, window // (2 * dilation))
        outs.append(o)
        lses.append(lse)
        dilations.append(dilation)
    y_attn = _attn_mix(outs, lses, dilations, seq_len)

    k_scale = jnp.concatenate([jnp.ones((mw,), F32), jnp.full((mw,), mhd ** -0.5, F32)])
    qk = _conv_silu(qkm, conv_w, conv_b, k_scale, batch, seq_len)
    gates_t = jnp.swapaxes(gates[:, :n_gate].reshape(batch, seq_len, n_gate), 1, 2)
    gp = _gate_prep(gates_t, gate_bias, n_mheads)
    grow = jnp.transpose(gp, (0, 2, 1, 3))
    gcol = jnp.transpose(gp, (0, 2, 3, 1))
    hf, hb = _mlstm_scan(qk, vom, grow, gcol, batch, seq_len, n_mheads)
    y_mlstm = _mlstm_post(hf, hb, vom, head_norm, n_mheads)

    (merged,) = _mm([y_attn, y_mlstm], [w_battn.astype(BF16), w_bmlstm.astype(BF16)], [(0, 0), (1, 1)],
                    [(bgate, ("mn", 0)), (bgate, ("mn", d))], _ep_merge, [BF16],
                    bm=1024, bn=1024, name="branch_merge")
    (y,) = _mm([merged], [w_out.astype(BF16)], [(0, 0)], [(x, "mn")], functools.partial(_ep_residual, 1.0), [F32],
               bm=1024, bn=1024, name="out_proj")
    return y


def kernel(x, ffn1_norm, ffn1_w_gate, ffn1_w_up, ffn1_w_down, mix_norm, w_in, mlstm_conv_w, mlstm_conv_b, mlstm_gate_bias, mlstm_head_norm, w_branch_attn, w_branch_mlstm, w_out, ffn2_norm, ffn2_w_gate, ffn2_w_up, ffn2_w_down, final_norm):
    batch, seq_len, d = x.shape
    h = x.reshape(batch * seq_len, d)
    for l in range(ffn1_norm.shape[0]):
        h = _ffn(h, ffn1_norm[l], ffn1_w_gate[l], ffn1_w_up[l], ffn1_w_down[l])
        h = _mixers(h, batch, seq_len, mix_norm[l], w_in[l], mlstm_conv_w[l], mlstm_conv_b[l], mlstm_gate_bias[l],
                    mlstm_head_norm[l], w_branch_attn[l], w_branch_mlstm[l], w_out[l])
        h = _ffn(h, ffn2_norm[l], ffn2_w_gate[l], ffn2_w_up[l], ffn2_w_down[l])
    return _rmsnorm(h, final_norm, F32).reshape(batch, seq_len, d)
```

```python
import functools

import numpy as np
import jax
import jax.numpy as jnp
from jax import lax
from jax.experimental import pallas as pl
from jax.experimental.pallas import tpu as pltpu

F32 = jnp.float32
BF16 = jnp.bfloat16

HEAD_DIM = 128
ATTN_GROUPS = ((128, 1), (512, 4), (2048, 16))
ROPE_DIM = HEAD_DIM // 4
ROPE_THETA = 500000.0
MLSTM_CHUNK = 128
MLSTM_HEADS_PER_STEP = 1
NORM_EPS = 1e-6
NEG_INF = -1e30

LANES = 128
VMEM_LIMIT_BYTES = 56 * 1024 * 1024


def _cparams(n_axes):
    return pltpu.CompilerParams(
        dimension_semantics=("arbitrary",) * n_axes, vmem_limit_bytes=VMEM_LIMIT_BYTES)


def _tile(n, pref):
    if n <= pref:
        return n
    t = (pref // LANES) * LANES
    while t >= LANES:
        if n % t == 0:
            return t
        t -= LANES
    return n


def _sigmoid(x):
    return 1.0 / (1.0 + jnp.exp(-x))


def _rmsnorm_kernel(x_ref, g_ref, o_ref):
    x = x_ref[...]
    ms = jnp.mean(x * x, axis=-1, keepdims=True)
    o_ref[...] = (x * lax.rsqrt(ms + NORM_EPS) * g_ref[...]).astype(o_ref.dtype)


def _rmsnorm(x, g, out_dtype):
    m, d = x.shape
    bm = _tile(m, 256)
    return pl.pallas_call(
        _rmsnorm_kernel,
        grid=(m // bm,),
        in_specs=[pl.BlockSpec((bm, d), lambda i: (i, 0)), pl.BlockSpec((1, d), lambda i: (0, 0))],
        out_specs=pl.BlockSpec((bm, d), lambda i: (i, 0)),
        out_shape=jax.ShapeDtypeStruct((m, d), out_dtype),
        compiler_params=_cparams(1),
        name="rmsnorm",
    )(x, g.reshape(1, d))


def _mm_kernel(*refs, n_lhs, shifts, pairs, n_extra, epilogue):
    lhs = refs[:n_lhs]
    pos = n_lhs
    rhs = []
    for shift in shifts:
        if shift:
            rhs.append(jnp.concatenate([refs[pos][:, shift:], refs[pos + 1][:, :shift]], axis=1).astype(BF16))
            pos += 2
        else:
            rhs.append(refs[pos][...].astype(BF16))
            pos += 1
    extras = refs[pos:pos + n_extra]
    outs = refs[pos + n_extra:]
    accs = [jnp.dot(lhs[a][...], rhs[b], preferred_element_type=F32) for a, b in pairs]
    res = epilogue(accs, [e[...] for e in extras])
    for o, r in zip(outs, res):
        o[...] = r.astype(o.dtype)


def _mm(lhs, rhs, pairs, extras, epilogue, out_dtypes, *, n, bm, bn, name):
    m = lhs[0].shape[0]
    bm = _tile(m, bm)
    bases = [c0 - c0 % LANES for _, c0 in rhs]
    bn = _tile(int(functools.reduce(np.gcd, bases, n)), bn)
    in_specs = [pl.BlockSpec((bm, l.shape[1]), lambda i, j: (i, 0)) for l in lhs]
    rhs_args = []
    for (r, c0), base in zip(rhs, bases):
        for extra_block in range(2 if c0 % LANES else 1):
            in_specs.append(pl.BlockSpec((r.shape[0], bn), lambda i, j, off=base // bn + extra_block: (0, j + off)))
            rhs_args.append(r)
    for arr, kind in extras:
        if kind == "mn" or isinstance(kind, tuple):
            off = 0 if kind == "mn" else kind[1] // bn
            in_specs.append(pl.BlockSpec((bm, bn), lambda i, j, off=off: (i, j + off)))
        else:
            in_specs.append(pl.BlockSpec((1, bn), lambda i, j: (0, j)))
    kern = functools.partial(
        _mm_kernel, n_lhs=len(lhs), shifts=tuple(c0 % LANES for _, c0 in rhs), pairs=tuple(pairs),
        n_extra=len(extras), epilogue=epilogue)
    return pl.pallas_call(
        kern,
        grid=(m // bm, n // bn),
        in_specs=in_specs,
        out_specs=[pl.BlockSpec((bm, bn), lambda i, j: (i, j)) for _ in out_dtypes],
        out_shape=[jax.ShapeDtypeStruct((m, n), dt) for dt in out_dtypes],
        compiler_params=_cparams(2),
        name=name,
    )(*lhs, *rhs_args, *[e for e, _ in extras])


def _ep_identity(accs, extras):
    return [accs[0]]


def _ep_sigmoid(accs, extras):
    return [_sigmoid(accs[0])]


def _ep_swiglu(accs, extras):
    g, u = accs
    return [g * _sigmoid(g) * u]


def _ep_rotary(accs, extras):
    acc = accs[0]
    c, s1, s2 = extras
    pieces = []
    for h in range(acc.shape[1] // HEAD_DIM):
        x = acc[:, h * HEAD_DIM:(h + 1) * HEAD_DIM]
        half = ROPE_DIM // 2
        pieces.append(x * c + pltpu.roll(x, half, 1) * s1 + pltpu.roll(x, HEAD_DIM - half, 1) * s2)
    return [jnp.concatenate(pieces, axis=1) if len(pieces) > 1 else pieces[0]]


def _ep_merge(accs, extras):
    ya, ym = accs
    ga, gm = extras
    return [ga.astype(F32) * ya + gm.astype(F32) * ym]


def _ep_residual(scale, accs, extras):
    return [extras[0] + scale * accs[0]]


def _attn_kernel(q_ref, kp_ref, kc_ref, kn_ref, vp_ref, vc_ref, vn_ref, o_ref, lse_ref, kx_ref, vx_ref,
                 *, bs, qb, reach, seq, n_heads):
    n = pl.program_id(2)
    kx_ref[0:reach, :] = kp_ref[...]
    kx_ref[reach:reach + bs, :] = kc_ref[...]
    kx_ref[reach + bs:, :] = kn_ref[...]
    vx_ref[0:reach, :] = vp_ref[...]
    vx_ref[reach:reach + bs, :] = vc_ref[...]
    vx_ref[reach + bs:, :] = vn_ref[...]

    kw = qb + 2 * reach
    row = lax.broadcasted_iota(jnp.int32, (qb, kw), 0)
    col = lax.broadcasted_iota(jnp.int32, (qb, kw), 1)
    delta = col - row

    def body(a, carry):
        r0 = pl.multiple_of(a * qb, qb)
        kpos = n * bs + r0 - reach + col
        mask = (delta >= 0) & (delta <= 2 * reach) & (kpos >= 0) & (kpos < seq)
        for h in range(n_heads):
            cs = slice(h * HEAD_DIM, (h + 1) * HEAD_DIM)
            q = q_ref[pl.ds(r0, qb), cs]
            k = kx_ref[pl.ds(r0, kw), cs]
            v = vx_ref[pl.ds(r0, kw), cs]
            s = lax.dot_general(q, k, (((1,), (1,)), ((), ())), preferred_element_type=F32)
            s = jnp.where(mask, s, NEG_INF)
            m = jnp.max(s, axis=1, keepdims=True)
            p = jnp.exp(s - m)
            l = jnp.sum(p, axis=1, keepdims=True)
            o = jnp.dot(p.astype(BF16), v, preferred_element_type=F32)
            o_ref[h, pl.ds(r0, qb), :] = o * (1.0 / l)
            lse_ref[h, pl.ds(r0, qb), :] = jnp.broadcast_to(m + jnp.log(l), (qb, HEAD_DIM))
        return carry

    lax.fori_loop(0, bs // qb, body, 0)


def _attn_proj_kernel(*refs, rotary, dilation):
    if rotary:
        h_ref, w_ref, c_ref, s1_ref, s2_ref, o_ref = refs[:6]
    else:
        h_ref, w_ref, o_ref = refs[:3]
    acc = jnp.dot(h_ref[...], w_ref[...].astype(BF16), preferred_element_type=F32)
    if rotary:
        (acc,) = _ep_rotary([acc], [c_ref[...], s1_ref[...], s2_ref[...]])
    if dilation == 1:
        o_ref[...] = acc.astype(o_ref.dtype)
    else:
        scr = refs[-1]
        rows = acc.shape[0] // dilation
        for c in range(acc.shape[1] // LANES):
            scr[c] = acc[:, c * LANES:(c + 1) * LANES]
        for r in range(dilation):
            for c in range(acc.shape[1] // LANES):
                o_ref[r, :, c * LANES:(c + 1) * LANES] = scr[c, pl.ds(r, rows, stride=dilation), :].astype(o_ref.dtype)


def _attn_proj(h, w, c0, wg, tables, batch, seq_len, dilation, name):
    m, d_model = h.shape
    bm = _tile(seq_len, 1024)
    bn = _tile(int(np.gcd(wg, c0)), 512)
    nbt = seq_len // bm
    assert bm % (dilation * 16) == 0 and bn % HEAD_DIM == 0
    in_specs = [pl.BlockSpec((bm, d_model), lambda i, j: (i, 0)),
                pl.BlockSpec((d_model, bn), lambda i, j: (0, j + c0 // bn))]
    args = [h, w]
    for tab in tables:
        in_specs.append(pl.BlockSpec((bm, LANES), lambda i, j: (i % nbt, 0)))
        args.append(tab)
    if dilation == 1:
        out_spec = pl.BlockSpec((None, None, bm, bn), lambda i, j: (i // nbt, 0, i % nbt, j))
        scratch = []
    else:
        out_spec = pl.BlockSpec((None, dilation, bm // dilation, bn), lambda i, j: (i // nbt, 0, i % nbt, j))
        scratch = [pltpu.VMEM((bn // LANES, bm, LANES), F32)]
    return pl.pallas_call(
        functools.partial(_attn_proj_kernel, rotary=bool(tables), dilation=dilation),
        grid=(m // bm, wg // bn),
        in_specs=in_specs,
        out_specs=out_spec,
        out_shape=jax.ShapeDtypeStruct((batch, dilation, seq_len // dilation, wg), BF16),
        scratch_shapes=scratch,
        compiler_params=_cparams(2),
        name=name,
    )(*args)


def _attn_group(q, k, v, group, reach):
    batch, dilation, s_len, w = q.shape
    n_heads = w // HEAD_DIM
    qb = 2 * reach
    bs = min(512, s_len)
    assert s_len % bs == 0 and bs % qb == 0
    hb = bs // reach
    n_hblk = s_len // reach

    main = pl.BlockSpec((None, None, bs, w), lambda b, r, n: (b, r, n, 0))
    prev = pl.BlockSpec((None, None, reach, w), lambda b, r, n: (b, r, jnp.maximum(n * hb - 1, 0), 0))
    nxt = pl.BlockSpec((None, None, reach, w), lambda b, r, n: (b, r, jnp.minimum((n + 1) * hb, n_hblk - 1), 0))
    out_spec = pl.BlockSpec((None, None, n_heads, bs, HEAD_DIM), lambda b, r, n: (b, r, 0, n, 0))
    kern = functools.partial(_attn_kernel, bs=bs, qb=qb, reach=reach, seq=s_len, n_heads=n_heads)
    return pl.pallas_call(
        kern,
        grid=(batch, dilation, s_len // bs),
        in_specs=[main, prev, main, nxt, prev, main, nxt],
        out_specs=[out_spec, out_spec],
        out_shape=[jax.ShapeDtypeStruct((batch, dilation, n_heads, s_len, HEAD_DIM), F32)] * 2,
        scratch_shapes=[pltpu.VMEM((bs + 2 * reach, w), BF16), pltpu.VMEM((bs + 2 * reach, w), BF16)],
        compiler_params=_cparams(3),
        name=f"dilated_attn_g{group}",
    )(q, k, k, k, v, v, v)


def _attn_mix_kernel(*refs, dilations, bt):
    n_groups = len(dilations)
    o_refs = refs[:n_groups]
    l_refs = refs[n_groups:2 * n_groups]
    y_ref = refs[2 * n_groups]
    scr = refs[2 * n_groups + 1]
    n_heads = scr.shape[0]
    dmax = max(dilations)
    rows = bt // dmax
    for r in range(dmax):
        for h in range(n_heads):
            def rd(ref, d):
                return ref[r % d, h, pl.ds(r // d, rows, stride=dmax // d), :]
            lses = [rd(ref, d) for ref, d in zip(l_refs, dilations)]
            mx = functools.reduce(jnp.maximum, lses)
            es = [jnp.exp(l - mx) for l in lses]
            den = functools.reduce(lambda a, b: a + b, es)
            num = functools.reduce(lambda a, b: a + b, [e * rd(ref, d) for e, ref, d in zip(es, o_refs, dilations)])
            scr[h, pl.ds(r, rows, stride=dmax), :] = num / den
    for h in range(n_heads):
        y_ref[:, h * HEAD_DIM:(h + 1) * HEAD_DIM] = scr[h].astype(y_ref.dtype)


def _attn_mix(outs, lses, dilations, seq_len):
    batch, _, n_heads, _, hd = outs[0].shape
    dmax = max(dilations)
    bt = _tile(seq_len, 512)
    assert bt % (8 * dmax) == 0 and all(dmax % d == 0 for d in dilations)
    nbt = seq_len // bt
    specs = [pl.BlockSpec((None, d, n_heads, bt // d, hd), lambda b, i: (b, 0, 0, i, 0)) for d in dilations]
    return pl.pallas_call(
        functools.partial(_attn_mix_kernel, dilations=tuple(dilations), bt=bt),
        grid=(batch, nbt),
        in_specs=specs + specs,
        out_specs=pl.BlockSpec((bt, n_heads * hd), lambda b, i: (b * nbt + i, 0)),
        out_shape=jax.ShapeDtypeStruct((batch * seq_len, n_heads * hd), BF16),
        scratch_shapes=[pltpu.VMEM((n_heads, bt, hd), F32)],
        compiler_params=_cparams(2),
        name="attn_mix",
    )(*outs, *lses)


def _conv_kernel(xp_ref, xc_ref, xn_ref, w_ref, b_ref, s_ref, o_ref, xe_ref, *, bt, n_tblk, ksize, halo):
    t = pl.program_id(1)
    left = (ksize - 1) // 2
    prev = xp_ref[...].astype(F32)
    nxt = xn_ref[...].astype(F32)
    xe_ref[0:halo, :] = jnp.where(t > 0, prev, 0.0)
    xe_ref[halo:halo + bt, :] = xc_ref[...].astype(F32)
    xe_ref[halo + bt:, :] = jnp.where(t < n_tblk - 1, nxt, 0.0)
    acc = xe_ref[pl.ds(halo - left, bt), :] * w_ref[0:1, :]
    for j in range(1, ksize):
        acc = acc + xe_ref[pl.ds(halo - left + j, bt), :] * w_ref[j:j + 1, :]
    y = acc + b_ref[...]
    o_ref[...] = (y * _sigmoid(y) * s_ref[...]).astype(o_ref.dtype)


def _conv_silu(x, w, b, scale, batch, seq_len):
    m, c = x.shape
    ksize = w.shape[0]
    halo = 8
    bt = _tile(seq_len, 512)
    bc = _tile(c, 512)
    n_tblk = seq_len // bt
    hb = bt // halo
    n_hblk = seq_len // halo
    x3 = x.reshape(batch, seq_len, c)
    cur = pl.BlockSpec((None, bt, bc), lambda b_, t, j: (b_, t, j))
    prev = pl.BlockSpec((None, halo, bc), lambda b_, t, j: (b_, jnp.maximum(t * hb - 1, 0), j))
    nxt = pl.BlockSpec((None, halo, bc), lambda b_, t, j: (b_, jnp.minimum((t + 1) * hb, n_hblk - 1), j))
    wspec = pl.BlockSpec((ksize, bc), lambda b_, t, j: (0, j))
    rspec = pl.BlockSpec((1, bc), lambda b_, t, j: (0, j))
    out = pl.pallas_call(
        functools.partial(_conv_kernel, bt=bt, n_tblk=n_tblk, ksize=ksize, halo=halo),
        grid=(batch, n_tblk, c // bc),
        in_specs=[prev, cur, nxt, wspec, rspec, rspec],
        out_specs=cur,
        out_shape=jax.ShapeDtypeStruct((batch, seq_len, c), BF16),
        scratch_shapes=[pltpu.VMEM((bt + 2 * halo, bc), F32)],
        compiler_params=_cparams(3),
        name="mlstm_conv_silu",
    )(x3, x3, x3, w, b.reshape(1, c), scale.reshape(1, c))
    return out.reshape(m, c)


def _gate_prep_kernel(g_ref, b_ref, o_ref, *, n_heads, chunk):
    x = g_ref[...] + b_ref[...]
    h = n_heads
    t = x.shape[1]

    def log_sigmoid(z):
        return jnp.minimum(z, 0.0) - jnp.log(1.0 + jnp.exp(-jnp.abs(z)))

    pos = lax.broadcasted_iota(jnp.int32, (h, t), 1) & (chunk - 1)

    def chunk_cumsum(z, reverse):
        shift = 1
        while shift < chunk:
            if reverse:
                moved = pltpu.roll(z, t - shift, 1)
                keep = pos < chunk - shift
            else:
                moved = pltpu.roll(z, shift, 1)
                keep = pos >= shift
            z = z + jnp.where(keep, moved, 0.0)
            shift *= 2
        return z

    o_ref[0] = chunk_cumsum(log_sigmoid(x[h:2 * h]), False)
    o_ref[1] = x[0:h]
    o_ref[2] = chunk_cumsum(log_sigmoid(x[3 * h:4 * h]), True)
    o_ref[3] = x[2 * h:3 * h]


def _gate_prep(gates_t, bias, n_heads):
    batch, rows, t = gates_t.shape
    return pl.pallas_call(
        functools.partial(_gate_prep_kernel, n_heads=n_heads, chunk=MLSTM_CHUNK),
        grid=(batch,),
        in_specs=[pl.BlockSpec((None, rows, t), lambda b: (b, 0, 0)), pl.BlockSpec((rows, 1), lambda b: (0, 0))],
        out_specs=pl.BlockSpec((None, 4, n_heads, t), lambda b: (b, 0, 0, 0)),
        out_shape=jax.ShapeDtypeStruct((batch, 4, n_heads, t), F32),
        compiler_params=_cparams(1),
        name="mlstm_gate_prep",
    )(gates_t, bias.reshape(rows, 1))


def _mlstm_direction(q_ref, k_ref, v_ref, grow_ref, gcol_ref, h_ref, c_ref, n_ref, m_ref, *, reverse, kind):
    L = q_ref.shape[0]
    q = q_ref[...]
    k = k_ref[...]
    v = v_ref[...]
    b_row = grow_ref[kind:kind + 1, :]
    i_row = grow_ref[kind + 1:kind + 2, :]
    b_col = gcol_ref[:, kind:kind + 1]
    i_col = gcol_ref[:, kind + 1:kind + 2]
    m_prev = m_ref[...]
    c_prev = c_ref[...]
    n_prev = n_ref[...]

    li = lax.broadcasted_iota(jnp.int32, (L, L), 0)
    si = lax.broadcasted_iota(jnp.int32, (L, L), 1)
    visible = (si >= li) if reverse else (si <= li)
    dmat = jnp.where(visible, b_col - b_row + i_row, NEG_INF)
    inter = b_col + m_prev
    m_t = jnp.maximum(inter, jnp.max(dmat, axis=1, keepdims=True))
    w_inter = jnp.exp(inter - m_t)
    w_intra = jnp.exp(dmat - m_t)
    qk = lax.dot_general(q, k, (((1,), (1,)), ((), ())), preferred_element_type=F32) * w_intra
    num = w_inter * jnp.dot(q, c_prev.astype(BF16), preferred_element_type=F32) + jnp.dot(
        qk.astype(BF16), v, preferred_element_type=F32)
    qn = jnp.sum(q.astype(F32) * n_prev, axis=1, keepdims=True)
    den = w_inter * qn + jnp.sum(qk, axis=1, keepdims=True)
    h_ref[...] = num / jnp.maximum(jnp.abs(den), jnp.exp(-m_t))

    b_last = b_col[0:1, :] if reverse else b_col[L - 1:L, :]
    g = b_last - b_col + i_col
    m_new = jnp.maximum(b_last + m_prev, jnp.max(g, axis=0, keepdims=True))
    decay = jnp.exp(b_last + m_prev - m_new)
    kw = k.astype(F32) * jnp.exp(g - m_new)
    c_ref[...] = decay * c_prev + lax.dot_general(
        kw.astype(BF16), v, (((0,), (0,)), ((), ())), preferred_element_type=F32)
    n_ref[...] = decay * n_prev + jnp.sum(kw, axis=0, keepdims=True)
    m_ref[...] = m_new


def _mlstm_kernel(qf, kf, vf, growf, gcolf, qb, kb, vb, growb, gcolb, hf_ref, hb_ref,
                  cf, nf, mf, cb, nb, mb, *, heads_per_step):
    @pl.when(pl.program_id(2) == 0)
    def _():
        for r in (cf, nf, mf, cb, nb, mb):
            r[...] = jnp.zeros(r.shape, r.dtype)

    d = qf.shape[1] // heads_per_step
    for hh in range(heads_per_step):
        cs = slice(hh * d, (hh + 1) * d)
        _mlstm_direction(qf.at[:, cs], kf.at[:, cs], vf.at[:, cs], growf.at[hh], gcolf.at[hh], hf_ref.at[:, cs],
                         cf.at[hh], nf.at[hh], mf.at[hh], reverse=False, kind=0)
        _mlstm_direction(qb.at[:, cs], kb.at[:, cs], vb.at[:, cs], growb.at[hh], gcolb.at[hh], hb_ref.at[:, cs],
                         cb.at[hh], nb.at[hh], mb.at[hh], reverse=True, kind=2)


def _mlstm_scan(qk, vo, grow, gcol, batch, seq_len, n_heads):
    m, width2 = qk.shape
    width = width2 // 2
    d = width // n_heads
    L = MLSTM_CHUNK
    nc = seq_len // L
    hps = MLSTM_HEADS_PER_STEP if n_heads % MLSTM_HEADS_PER_STEP == 0 else 1
    n_hblk = n_heads // hps
    qk3 = qk.reshape(batch, seq_len, width2)
    vo3 = vo.reshape(batch, seq_len, width2)

    def seq_spec(rev, col0):
        return pl.BlockSpec((None, L, hps * d), lambda b, h, c: (b, nc - 1 - c if rev else c, h + col0))

    def row_spec(rev):
        return pl.BlockSpec((None, hps, 4, L), lambda b, h, c: (b, h, 0, nc - 1 - c if rev else c))

    def col_spec(rev):
        return pl.BlockSpec((None, hps, L, 4), lambda b, h, c: (b, h, nc - 1 - c if rev else c, 0))

    def direction(rev):
        return [seq_spec(rev, 0), seq_spec(rev, n_hblk), seq_spec(rev, 0), row_spec(rev), col_spec(rev)]

    hf, hb = pl.pallas_call(
        functools.partial(_mlstm_kernel, heads_per_step=hps),
        grid=(batch, n_hblk, nc),
        in_specs=direction(False) + direction(True),
        out_specs=[seq_spec(False, 0), seq_spec(True, 0)],
        out_shape=[jax.ShapeDtypeStruct((batch, seq_len, width), F32)] * 2,
        scratch_shapes=[pltpu.VMEM((hps, d, d), F32), pltpu.VMEM((hps, 1, d), F32), pltpu.VMEM((hps, 1, 1), F32)] * 2,
        compiler_params=_cparams(3),
        name="mlstm_scan",
    )(qk3, qk3, vo3, grow, gcol, qk3, qk3, vo3, grow, gcol)
    return hf.reshape(m, width), hb.reshape(m, width)


def _mlstm_post_kernel(hf_ref, hb_ref, o_ref, w_ref, y_ref, *, n_heads):
    d = hf_ref.shape[1] // n_heads
    for h in range(n_heads):
        cs = slice(h * d, (h + 1) * d)
        x = hf_ref[:, cs] + hb_ref[:, cs]
        mu = jnp.mean(x, axis=1, keepdims=True)
        xc = x - mu
        var = jnp.mean(xc * xc, axis=1, keepdims=True)
        y = xc * lax.rsqrt(var + NORM_EPS) * w_ref[:, cs]
        y_ref[:, cs] = (_sigmoid(o_ref[:, cs].astype(F32)) * y).astype(y_ref.dtype)


def _mlstm_post(hf, hb, vo, w, n_heads):
    m, width = hf.shape
    bm = _tile(m, 256)
    spec = pl.BlockSpec((bm, width), lambda i: (i, 0))
    return pl.pallas_call(
        functools.partial(_mlstm_post_kernel, n_heads=n_heads),
        grid=(m // bm,),
        in_specs=[spec, spec, pl.BlockSpec((bm, width), lambda i: (i, 1)), pl.BlockSpec((1, width), lambda i: (0, 0))],
        out_specs=spec,
        out_shape=jax.ShapeDtypeStruct((m, width), BF16),
        compiler_params=_cparams(1),
        name="mlstm_post",
    )(hf, hb, vo, w.reshape(1, width))


def _rope_tables(seq_len, scale):
    half = ROPE_DIM // 2
    inv_freq = ROPE_THETA ** (-jnp.arange(half, dtype=F32) * 2.0 / ROPE_DIM)
    ang = jnp.arange(seq_len).astype(F32)[:, None] * inv_freq[None, :]
    cos, sin = jnp.cos(ang), jnp.sin(ang)
    zeros = jnp.zeros((seq_len, half), F32)
    rest = HEAD_DIM - ROPE_DIM
    c = jnp.concatenate([cos, cos, jnp.ones((seq_len, rest), F32)], axis=1)
    s1 = jnp.concatenate([zeros, sin, jnp.zeros((seq_len, rest), F32)], axis=1)
    s2 = jnp.concatenate([-sin, zeros, jnp.zeros((seq_len, rest), F32)], axis=1)
    return [c * scale, s1 * scale, s2 * scale]


def _ffn(x, norm_g, w_gate, w_up, w_down):
    d, d_ff = w_gate.shape
    h = _rmsnorm(x, norm_g, BF16)
    (act,) = _mm([h], [(w_gate, 0), (w_up, 0)], [(0, 0), (0, 1)], [], _ep_swiglu, [BF16],
                 n=d_ff, bm=1024, bn=256, name="ffn_gate_up")
    (y,) = _mm([act], [(w_down.astype(BF16), 0)], [(0, 0)], [(x, "mn")], functools.partial(_ep_residual, 0.5), [F32],
               n=d, bm=512, bn=512, name="ffn_down")
    return y


def _mixers(x, batch, seq_len, mix_norm, w_in, conv_w, conv_b, gate_bias, head_norm, w_battn, w_bmlstm, w_out):
    m, d = x.shape
    attn_out = w_battn.shape[0]
    attn_w = len(ATTN_GROUPS) * attn_out
    mw = w_bmlstm.shape[0]
    n_mheads = gate_bias.shape[0] // 4
    mhd = mw // n_mheads
    n_gate = 4 * n_mheads
    offs = np.cumsum([0, attn_w, attn_w, attn_w, mw, mw, mw, mw, n_gate, d, d]).tolist()

    h = _rmsnorm(x, mix_norm, BF16)
    (qkm,) = _mm([h], [(w_in, offs[3])], [(0, 0)], [], _ep_identity, [BF16], n=2 * mw, bm=1024, bn=512,
                 name="proj_mlstm_qk")
    (vom,) = _mm([h], [(w_in, offs[5])], [(0, 0)], [], _ep_identity, [BF16], n=2 * mw, bm=1024, bn=512,
                 name="proj_mlstm_vo")
    (gates,) = _mm([h], [(w_in, offs[7])], [(0, 0)], [], _ep_identity, [F32], n=LANES, bm=1024, bn=LANES,
                   name="proj_mlstm_gates")
    (bgate,) = _mm([h], [(w_in, offs[8])], [(0, 0)], [], _ep_sigmoid, [BF16], n=2 * d, bm=1024, bn=512,
                   name="proj_branch_gates")

    q_tabs = _rope_tables(seq_len, HEAD_DIM ** -0.5)
    k_tabs = _rope_tables(seq_len, 1.0)
    outs, lses, dilations = [], [], []
    for g, (window, dilation) in enumerate(ATTN_GROUPS):
        c0 = g * attn_out
        qg = _attn_proj(h, w_in, offs[0] + c0, attn_out, q_tabs, batch, seq_len, dilation, f"proj_attn_q{g}")
        kg = _attn_proj(h, w_in, offs[1] + c0, attn_out, k_tabs, batch, seq_len, dilation, f"proj_attn_k{g}")
        vg = _attn_proj(h, w_in, offs[2] + c0, attn_out, [], batch, seq_len, dilation, f"proj_attn_v{g}")
        o, lse = _attn_group(qg, kg, vg, g, window // (2 * dilation))
        outs.append(o)
        lses.append(lse)
        dilations.append(dilation)
    y_attn = _attn_mix(outs, lses, dilations, seq_len)

    k_scale = jnp.concatenate([jnp.ones((mw,), F32), jnp.full((mw,), mhd ** -0.5, F32)])
    qk = _conv_silu(qkm, conv_w, conv_b, k_scale, batch, seq_len)
    gates_t = jnp.swapaxes(gates[:, :n_gate].reshape(batch, seq_len, n_gate), 1, 2)
    gp = _gate_prep(gates_t, gate_bias, n_mheads)
    grow = jnp.transpose(gp, (0, 2, 1, 3))
    gcol = jnp.transpose(gp, (0, 2, 3, 1))
    hf, hb = _mlstm_scan(qk, vom, grow, gcol, batch, seq_len, n_mheads)
    y_mlstm = _mlstm_post(hf, hb, vom, head_norm, n_mheads)

    (merged,) = _mm([y_attn, y_mlstm], [(w_battn, 0), (w_bmlstm, 0)], [(0, 0), (1, 1)],
                    [(bgate, ("mn", 0)), (bgate, ("mn", d))], _ep_merge, [BF16],
                    n=d, bm=1024, bn=512, name="branch_merge")
    (y,) = _mm([merged], [(w_out, 0)], [(0, 0)], [(x, "mn")], functools.partial(_ep_residual, 1.0), [F32],
               n=d, bm=1024, bn=512, name="out_proj")
    return y


def kernel(x, ffn1_norm, ffn1_w_gate, ffn1_w_up, ffn1_w_down, mix_norm, w_in, mlstm_conv_w, mlstm_conv_b, mlstm_gate_bias, mlstm_head_norm, w_branch_attn, w_branch_mlstm, w_out, ffn2_norm, ffn2_w_gate, ffn2_w_up, ffn2_w_down, final_norm):
    batch, seq_len, d = x.shape
    h = x.reshape(batch * seq_len, d)
    for l in range(ffn1_norm.shape[0]):
        h = _ffn(h, ffn1_norm[l], ffn1_w_gate[l], ffn1_w_up[l], ffn1_w_down[l])
        h = _mixers(h, batch, seq_len, mix_norm[l], w_in[l], mlstm_conv_w[l], mlstm_conv_b[l], mlstm_gate_bias[l],
                    mlstm_head_norm[l], w_branch_attn[l], w_branch_mlstm[l], w_out[l])
        h = _ffn(h, ffn2_norm[l], ffn2_w_gate[l], ffn2_w_up[l], ffn2_w_down[l])
    return _rmsnorm(h, final_norm, F32).reshape(batch, seq_len, d)
```

```python
import functools

import numpy as np
import jax
import jax.numpy as jnp
from jax import lax
from jax.experimental import pallas as pl
from jax.experimental.pallas import tpu as pltpu

F32 = jnp.float32
BF16 = jnp.bfloat16

HEAD_DIM = 128
ATTN_GROUPS = ((128, 1), (512, 4), (2048, 16))
ROPE_DIM = HEAD_DIM // 4
ROPE_THETA = 500000.0
MLSTM_CHUNK = 128
MLSTM_HEADS_PER_STEP = 1
MLSTM_CHUNKS_PER_STEP = 1
NORM_EPS = 1e-6
NEG_INF = -1e30

LANES = 128
VMEM_LIMIT_BYTES = 56 * 1024 * 1024


def _cparams(n_axes):
    return pltpu.CompilerParams(
        dimension_semantics=("arbitrary",) * n_axes, vmem_limit_bytes=VMEM_LIMIT_BYTES)


def _tile(n, pref):
    if n <= pref:
        return n
    t = (pref // LANES) * LANES
    while t >= LANES:
        if n % t == 0:
            return t
        t -= LANES
    return n


def _sigmoid(x):
    return 1.0 / (1.0 + jnp.exp(-x))


def _rmsnorm_kernel(x_ref, g_ref, o_ref):
    x = x_ref[...]
    ms = jnp.mean(x * x, axis=-1, keepdims=True)
    o_ref[...] = (x * lax.rsqrt(ms + NORM_EPS) * g_ref[...]).astype(o_ref.dtype)


def _rmsnorm(x, g, out_dtype):
    m, d = x.shape
    bm = _tile(m, 256)
    return pl.pallas_call(
        _rmsnorm_kernel,
        grid=(m // bm,),
        in_specs=[pl.BlockSpec((bm, d), lambda i: (i, 0)), pl.BlockSpec((1, d), lambda i: (0, 0))],
        out_specs=pl.BlockSpec((bm, d), lambda i: (i, 0)),
        out_shape=jax.ShapeDtypeStruct((m, d), out_dtype),
        compiler_params=_cparams(1),
        name="rmsnorm",
    )(x, g.reshape(1, d))


def _mm_kernel(*refs, n_lhs, rhs_t, pairs, n_extra, epilogue):
    lhs = refs[:n_lhs]
    n_rhs = len(rhs_t)
    rhs = refs[n_lhs:n_lhs + n_rhs]
    extras = refs[n_lhs + n_rhs:n_lhs + n_rhs + n_extra]
    outs = refs[n_lhs + n_rhs + n_extra:]
    accs = []
    for a, b in pairs:
        contract = (((1,), (1,)), ((), ())) if rhs_t[b] else (((1,), (0,)), ((), ()))
        accs.append(lax.dot_general(lhs[a][...], rhs[b][...].astype(BF16), contract, preferred_element_type=F32))
    res = epilogue(accs, [e[...] for e in extras])
    for o, r in zip(outs, res):
        o[...] = r.astype(o.dtype)


def _mm(lhs, rhs, pairs, extras, epilogue, out_dtypes, *, n, bm, bn, name):
    m = lhs[0].shape[0]
    bm = _tile(m, bm)
    bn = _tile(int(functools.reduce(np.gcd, [c0 for _, c0, _ in rhs], n)), bn)
    in_specs = [pl.BlockSpec((bm, l.shape[1]), lambda i, j: (i, 0)) for l in lhs]
    for r, c0, transposed in rhs:
        if transposed:
            in_specs.append(pl.BlockSpec((bn, r.shape[1]), lambda i, j, off=c0 // bn: (j + off, 0)))
        else:
            in_specs.append(pl.BlockSpec((r.shape[0], bn), lambda i, j, off=c0 // bn: (0, j + off)))
    for arr, kind in extras:
        if kind == "mn" or isinstance(kind, tuple):
            off = 0 if kind == "mn" else kind[1] // bn
            in_specs.append(pl.BlockSpec((bm, bn), lambda i, j, off=off: (i, j + off)))
        else:
            in_specs.append(pl.BlockSpec((1, bn), lambda i, j: (0, j)))
    kern = functools.partial(
        _mm_kernel, n_lhs=len(lhs), rhs_t=tuple(t for _, _, t in rhs), pairs=tuple(pairs),
        n_extra=len(extras), epilogue=epilogue)
    return pl.pallas_call(
        kern,
        grid=(m // bm, n // bn),
        in_specs=in_specs,
        out_specs=[pl.BlockSpec((bm, bn), lambda i, j: (i, j)) for _ in out_dtypes],
        out_shape=[jax.ShapeDtypeStruct((m, n), dt) for dt in out_dtypes],
        compiler_params=_cparams(2),
        name=name,
    )(*lhs, *[r for r, _, _ in rhs], *[e for e, _ in extras])


def _ep_identity(accs, extras):
    return [accs[0]]


def _ep_sigmoid(accs, extras):
    return [_sigmoid(accs[0])]


def _ep_swiglu(accs, extras):
    g, u = accs
    return [g * _sigmoid(g) * u]


def _ep_rotary(accs, extras):
    acc = accs[0]
    c, s1, s2 = extras
    pieces = []
    for h in range(acc.shape[1] // HEAD_DIM):
        x = acc[:, h * HEAD_DIM:(h + 1) * HEAD_DIM]
        half = ROPE_DIM // 2
        pieces.append(x * c + pltpu.roll(x, half, 1) * s1 + pltpu.roll(x, HEAD_DIM - half, 1) * s2)
    return [jnp.concatenate(pieces, axis=1) if len(pieces) > 1 else pieces[0]]


def _ep_merge(accs, extras):
    ya, ym = accs
    ga, gm = extras
    return [ga.astype(F32) * ya + gm.astype(F32) * ym]


def _ep_residual(scale, accs, extras):
    return [extras[0] + scale * accs[0]]


def _attn_kernel(q_ref, kp_ref, kc_ref, kn_ref, vp_ref, vc_ref, vn_ref, o_ref, lse_ref, kx_ref, vx_ref,
                 *, bs, qb, reach, seq, n_heads):
    n = pl.program_id(2)
    kx_ref[0:reach, :] = kp_ref[...]
    kx_ref[reach:reach + bs, :] = kc_ref[...]
    kx_ref[reach + bs:, :] = kn_ref[...]
    vx_ref[0:reach, :] = vp_ref[...]
    vx_ref[reach:reach + bs, :] = vc_ref[...]
    vx_ref[reach + bs:, :] = vn_ref[...]

    kw = qb + 2 * reach
    row = lax.broadcasted_iota(jnp.int32, (qb, kw), 0)
    col = lax.broadcasted_iota(jnp.int32, (qb, kw), 1)
    delta = col - row

    def body(a, carry):
        r0 = pl.multiple_of(a * qb, qb)
        kpos = n * bs + r0 - reach + col
        mask = (delta >= 0) & (delta <= 2 * reach) & (kpos >= 0) & (kpos < seq)
        for h in range(n_heads):
            cs = slice(h * HEAD_DIM, (h + 1) * HEAD_DIM)
            q = q_ref[pl.ds(r0, qb), cs]
            k = kx_ref[pl.ds(r0, kw), cs]
            v = vx_ref[pl.ds(r0, kw), cs]
            s = lax.dot_general(q, k, (((1,), (1,)), ((), ())), preferred_element_type=F32)
            s = jnp.where(mask, s, NEG_INF)
            m = jnp.max(s, axis=1, keepdims=True)
            p = jnp.exp(s - m)
            l = jnp.sum(p, axis=1, keepdims=True)
            o = jnp.dot(p.astype(BF16), v, preferred_element_type=F32)
            o_ref[h, pl.ds(r0, qb), :] = o * (1.0 / l)
            lse_ref[h, pl.ds(r0, qb), :] = jnp.broadcast_to(m + jnp.log(l), (qb, HEAD_DIM))
        return carry

    lax.fori_loop(0, bs // qb, body, 0)


def _attn_proj_kernel(*refs, rotary, dilation):
    if rotary:
        h_ref, w_ref, c_ref, s1_ref, s2_ref, o_ref = refs[:6]
    else:
        h_ref, w_ref, o_ref = refs[:3]
    acc = lax.dot_general(h_ref[...], w_ref[...], (((1,), (1,)), ((), ())), preferred_element_type=F32)
    if rotary:
        (acc,) = _ep_rotary([acc], [c_ref[...], s1_ref[...], s2_ref[...]])
    if dilation == 1:
        o_ref[...] = acc.astype(o_ref.dtype)
    else:
        scr = refs[-1]
        rows = acc.shape[0] // dilation
        for c in range(acc.shape[1] // LANES):
            scr[c] = acc[:, c * LANES:(c + 1) * LANES]
        for r in range(dilation):
            for c in range(acc.shape[1] // LANES):
                o_ref[r, :, c * LANES:(c + 1) * LANES] = scr[c, pl.ds(r, rows, stride=dilation), :].astype(o_ref.dtype)


def _attn_proj(h, w_t, c0, wg, tables, batch, seq_len, dilation, name):
    m, d_model = h.shape
    bm = _tile(seq_len, 1024)
    bn = _tile(int(np.gcd(wg, c0)), 1024)
    nbt = seq_len // bm
    assert bm % (dilation * 16) == 0 and bn % HEAD_DIM == 0
    in_specs = [pl.BlockSpec((bm, d_model), lambda i, j: (i, 0)),
                pl.BlockSpec((bn, d_model), lambda i, j: (j + c0 // bn, 0))]
    args = [h, w_t]
    for tab in tables:
        in_specs.append(pl.BlockSpec((bm, LANES), lambda i, j: (i % nbt, 0)))
        args.append(tab)
    if dilation == 1:
        out_spec = pl.BlockSpec((None, None, bm, bn), lambda i, j: (i // nbt, 0, i % nbt, j))
        scratch = []
    else:
        out_spec = pl.BlockSpec((None, dilation, bm // dilation, bn), lambda i, j: (i // nbt, 0, i % nbt, j))
        scratch = [pltpu.VMEM((bn // LANES, bm, LANES), F32)]
    return pl.pallas_call(
        functools.partial(_attn_proj_kernel, rotary=bool(tables), dilation=dilation),
        grid=(m // bm, wg // bn),
        in_specs=in_specs,
        out_specs=out_spec,
        out_shape=jax.ShapeDtypeStruct((batch, dilation, seq_len // dilation, wg), BF16),
        scratch_shapes=scratch,
        compiler_params=_cparams(2),
        name=name,
    )(*args)


def _attn_group(q, k, v, group, reach):
    batch, dilation, s_len, w = q.shape
    n_heads = w // HEAD_DIM
    qb = 2 * reach
    bs = min(512, s_len)
    assert s_len % bs == 0 and bs % qb == 0
    hb = bs // reach
    n_hblk = s_len // reach

    main = pl.BlockSpec((None, None, bs, w), lambda b, r, n: (b, r, n, 0))
    prev = pl.BlockSpec((None, None, reach, w), lambda b, r, n: (b, r, jnp.maximum(n * hb - 1, 0), 0))
    nxt = pl.BlockSpec((None, None, reach, w), lambda b, r, n: (b, r, jnp.minimum((n + 1) * hb, n_hblk - 1), 0))
    out_spec = pl.BlockSpec((None, None, n_heads, bs, HEAD_DIM), lambda b, r, n: (b, r, 0, n, 0))
    kern = functools.partial(_attn_kernel, bs=bs, qb=qb, reach=reach, seq=s_len, n_heads=n_heads)
    return pl.pallas_call(
        kern,
        grid=(batch, dilation, s_len // bs),
        in_specs=[main, prev, main, nxt, prev, main, nxt],
        out_specs=[out_spec, out_spec],
        out_shape=[jax.ShapeDtypeStruct((batch, dilation, n_heads, s_len, HEAD_DIM), F32)] * 2,
        scratch_shapes=[pltpu.VMEM((bs + 2 * reach, w), BF16), pltpu.VMEM((bs + 2 * reach, w), BF16)],
        compiler_params=_cparams(3),
        name=f"dilated_attn_g{group}",
    )(q, k, k, k, v, v, v)


def _attn_mix_kernel(*refs, dilations, bt):
    n_groups = len(dilations)
    o_refs = refs[:n_groups]
    l_refs = refs[n_groups:2 * n_groups]
    y_ref = refs[2 * n_groups]
    scr = refs[2 * n_groups + 1]
    n_heads = scr.shape[0]
    dmax = max(dilations)
    rows = bt // dmax
    for r in range(dmax):
        for h in range(n_heads):
            def rd(ref, d):
                return ref[r % d, h, pl.ds(r // d, rows, stride=dmax // d), :]
            lses = [rd(ref, d) for ref, d in zip(l_refs, dilations)]
            mx = functools.reduce(jnp.maximum, lses)
            es = [jnp.exp(l - mx) for l in lses]
            den = functools.reduce(lambda a, b: a + b, es)
            num = functools.reduce(lambda a, b: a + b, [e * rd(ref, d) for e, ref, d in zip(es, o_refs, dilations)])
            scr[h, pl.ds(r, rows, stride=dmax), :] = num / den
    for h in range(n_heads):
        y_ref[:, h * HEAD_DIM:(h + 1) * HEAD_DIM] = scr[h].astype(y_ref.dtype)


def _attn_mix(outs, lses, dilations, seq_len):
    batch, _, n_heads, _, hd = outs[0].shape
    dmax = max(dilations)
    bt = _tile(seq_len, 512)
    assert bt % (8 * dmax) == 0 and all(dmax % d == 0 for d in dilations)
    nbt = seq_len // bt
    specs = [pl.BlockSpec((None, d, n_heads, bt // d, hd), lambda b, i: (b, 0, 0, i, 0)) for d in dilations]
    return pl.pallas_call(
        functools.partial(_attn_mix_kernel, dilations=tuple(dilations), bt=bt),
        grid=(batch, nbt),
        in_specs=specs + specs,
        out_specs=pl.BlockSpec((bt, n_heads * hd), lambda b, i: (b * nbt + i, 0)),
        out_shape=jax.ShapeDtypeStruct((batch * seq_len, n_heads * hd), BF16),
        scratch_shapes=[pltpu.VMEM((n_heads, bt, hd), F32)],
        compiler_params=_cparams(2),
        name="attn_mix",
    )(*outs, *lses)


def _conv_kernel(xp_ref, xc_ref, xn_ref, w_ref, b_ref, s_ref, o_ref, xe_ref, *, bt, n_tblk, ksize, halo):
    t = pl.program_id(1)
    left = (ksize - 1) // 2
    prev = xp_ref[...].astype(F32)
    nxt = xn_ref[...].astype(F32)
    xe_ref[0:halo, :] = jnp.where(t > 0, prev, 0.0)
    xe_ref[halo:halo + bt, :] = xc_ref[...].astype(F32)
    xe_ref[halo + bt:, :] = jnp.where(t < n_tblk - 1, nxt, 0.0)
    acc = xe_ref[pl.ds(halo - left, bt), :] * w_ref[0:1, :]
    for j in range(1, ksize):
        acc = acc + xe_ref[pl.ds(halo - left + j, bt), :] * w_ref[j:j + 1, :]
    y = acc + b_ref[...]
    o_ref[...] = (y * _sigmoid(y) * s_ref[...]).astype(o_ref.dtype)


def _conv_silu(x, w, b, scale, batch, seq_len):
    m, c = x.shape
    ksize = w.shape[0]
    halo = 8
    bt = _tile(seq_len, 512)
    bc = _tile(c, 512)
    n_tblk = seq_len // bt
    hb = bt // halo
    n_hblk = seq_len // halo
    x3 = x.reshape(batch, seq_len, c)
    cur = pl.BlockSpec((None, bt, bc), lambda b_, t, j: (b_, t, j))
    prev = pl.BlockSpec((None, halo, bc), lambda b_, t, j: (b_, jnp.maximum(t * hb - 1, 0), j))
    nxt = pl.BlockSpec((None, halo, bc), lambda b_, t, j: (b_, jnp.minimum((t + 1) * hb, n_hblk - 1), j))
    wspec = pl.BlockSpec((ksize, bc), lambda b_, t, j: (0, j))
    rspec = pl.BlockSpec((1, bc), lambda b_, t, j: (0, j))
    out = pl.pallas_call(
        functools.partial(_conv_kernel, bt=bt, n_tblk=n_tblk, ksize=ksize, halo=halo),
        grid=(batch, n_tblk, c // bc),
        in_specs=[prev, cur, nxt, wspec, rspec, rspec],
        out_specs=cur,
        out_shape=jax.ShapeDtypeStruct((batch, seq_len, c), BF16),
        scratch_shapes=[pltpu.VMEM((bt + 2 * halo, bc), F32)],
        compiler_params=_cparams(3),
        name="mlstm_conv_silu",
    )(x3, x3, x3, w, b.reshape(1, c), scale.reshape(1, c))
    return out.reshape(m, c)


def _gate_prep_kernel(g_ref, b_ref, o_ref, *, n_heads, chunk):
    x = g_ref[...] + b_ref[...]
    h = n_heads
    t = x.shape[1]

    def log_sigmoid(z):
        return jnp.minimum(z, 0.0) - jnp.log(1.0 + jnp.exp(-jnp.abs(z)))

    pos = lax.broadcasted_iota(jnp.int32, (h, t), 1) & (chunk - 1)

    def chunk_cumsum(z, reverse):
        shift = 1
        while shift < chunk:
            if reverse:
                moved = pltpu.roll(z, t - shift, 1)
                keep = pos < chunk - shift
            else:
                moved = pltpu.roll(z, shift, 1)
                keep = pos >= shift
            z = z + jnp.where(keep, moved, 0.0)
            shift *= 2
        return z

    o_ref[0] = chunk_cumsum(log_sigmoid(x[h:2 * h]), False)
    o_ref[1] = x[0:h]
    o_ref[2] = chunk_cumsum(log_sigmoid(x[3 * h:4 * h]), True)
    o_ref[3] = x[2 * h:3 * h]


def _gate_prep(gates_t, bias, n_heads):
    batch, rows, t = gates_t.shape
    return pl.pallas_call(
        functools.partial(_gate_prep_kernel, n_heads=n_heads, chunk=MLSTM_CHUNK),
        grid=(batch,),
        in_specs=[pl.BlockSpec((None, rows, t), lambda b: (b, 0, 0)), pl.BlockSpec((rows, 1), lambda b: (0, 0))],
        out_specs=pl.BlockSpec((None, 4, n_heads, t), lambda b: (b, 0, 0, 0)),
        out_shape=jax.ShapeDtypeStruct((batch, 4, n_heads, t), F32),
        compiler_params=_cparams(1),
        name="mlstm_gate_prep",
    )(gates_t, bias.reshape(rows, 1))


def _mlstm_direction(q_ref, k_ref, v_ref, grow_ref, gcol_ref, h_ref, c_ref, n_ref, m_ref, *, reverse, kind):
    L = q_ref.shape[0]
    q = q_ref[...]
    k = k_ref[...]
    v = v_ref[...]
    b_row = grow_ref[kind:kind + 1, :]
    i_row = grow_ref[kind + 1:kind + 2, :]
    b_col = gcol_ref[:, kind:kind + 1]
    i_col = gcol_ref[:, kind + 1:kind + 2]
    m_prev = m_ref[...]
    c_prev = c_ref[...]
    n_prev = n_ref[...]

    li = lax.broadcasted_iota(jnp.int32, (L, L), 0)
    si = lax.broadcasted_iota(jnp.int32, (L, L), 1)
    visible = (si >= li) if reverse else (si <= li)
    dmat = jnp.where(visible, b_col - b_row + i_row, NEG_INF)
    inter = b_col + m_prev
    m_t = jnp.maximum(inter, jnp.max(dmat, axis=1, keepdims=True))
    w_inter = jnp.exp(inter - m_t)
    w_intra = jnp.exp(dmat - m_t)
    qk = lax.dot_general(q, k, (((1,), (1,)), ((), ())), preferred_element_type=F32) * w_intra
    num = w_inter * jnp.dot(q, c_prev.astype(BF16), preferred_element_type=F32) + jnp.dot(
        qk.astype(BF16), v, preferred_element_type=F32)
    qn = jnp.sum(q.astype(F32) * n_prev, axis=1, keepdims=True)
    den = w_inter * qn + jnp.sum(qk, axis=1, keepdims=True)
    h_ref[...] = num / jnp.maximum(jnp.abs(den), jnp.exp(-m_t))

    b_last = b_col[0:1, :] if reverse else b_col[L - 1:L, :]
    g = b_last - b_col + i_col
    m_new = jnp.maximum(b_last + m_prev, jnp.max(g, axis=0, keepdims=True))
    decay = jnp.exp(b_last + m_prev - m_new)
    kw = k.astype(F32) * jnp.exp(g - m_new)
    c_ref[...] = decay * c_prev + lax.dot_general(
        kw.astype(BF16), v, (((0,), (0,)), ((), ())), preferred_element_type=F32)
    n_ref[...] = decay * n_prev + jnp.sum(kw, axis=0, keepdims=True)
    m_ref[...] = m_new


def _mlstm_kernel(qf, kf, vf, growf, gcolf, qb, kb, vb, growb, gcolb, hf_ref, hb_ref,
                  cf, nf, mf, cb, nb, mb, *, heads_per_step, chunks_per_step):
    @pl.when(pl.program_id(2) == 0)
    def _():
        for r in (cf, nf, mf, cb, nb, mb):
            r[...] = jnp.zeros(r.shape, r.dtype)

    d = qf.shape[1] // heads_per_step
    L = qf.shape[0] // chunks_per_step
    for u in range(chunks_per_step):
        for hh in range(heads_per_step):
            cs = slice(hh * d, (hh + 1) * d)
            rf = slice(u * L, (u + 1) * L)
            rb = slice((chunks_per_step - 1 - u) * L, (chunks_per_step - u) * L)
            _mlstm_direction(qf.at[rf, cs], kf.at[rf, cs], vf.at[rf, cs], growf.at[hh, :, rf], gcolf.at[hh, rf, :],
                             hf_ref.at[rf, cs], cf.at[hh], nf.at[hh], mf.at[hh], reverse=False, kind=0)
            _mlstm_direction(qb.at[rb, cs], kb.at[rb, cs], vb.at[rb, cs], growb.at[hh, :, rb], gcolb.at[hh, rb, :],
                             hb_ref.at[rb, cs], cb.at[hh], nb.at[hh], mb.at[hh], reverse=True, kind=2)


def _mlstm_scan(qk, vo, grow, gcol, batch, seq_len, n_heads):
    m, width2 = qk.shape
    width = width2 // 2
    d = width // n_heads
    hps = MLSTM_HEADS_PER_STEP if n_heads % MLSTM_HEADS_PER_STEP == 0 else 1
    cps = MLSTM_CHUNKS_PER_STEP if seq_len % (MLSTM_CHUNK * MLSTM_CHUNKS_PER_STEP) == 0 else 1
    rows = MLSTM_CHUNK * cps
    nc = seq_len // rows
    n_hblk = n_heads // hps
    qk3 = qk.reshape(batch, seq_len, width2)
    vo3 = vo.reshape(batch, seq_len, width2)

    def seq_spec(rev, col0):
        return pl.BlockSpec((None, rows, hps * d), lambda b, h, c: (b, nc - 1 - c if rev else c, h + col0))

    def row_spec(rev):
        return pl.BlockSpec((None, hps, 4, rows), lambda b, h, c: (b, h, 0, nc - 1 - c if rev else c))

    def col_spec(rev):
        return pl.BlockSpec((None, hps, rows, 4), lambda b, h, c: (b, h, nc - 1 - c if rev else c, 0))

    def direction(rev):
        return [seq_spec(rev, 0), seq_spec(rev, n_hblk), seq_spec(rev, 0), row_spec(rev), col_spec(rev)]

    hf, hb = pl.pallas_call(
        functools.partial(_mlstm_kernel, heads_per_step=hps, chunks_per_step=cps),
        grid=(batch, n_hblk, nc),
        in_specs=direction(False) + direction(True),
        out_specs=[seq_spec(False, 0), seq_spec(True, 0)],
        out_shape=[jax.ShapeDtypeStruct((batch, seq_len, width), F32)] * 2,
        scratch_shapes=[pltpu.VMEM((hps, d, d), F32), pltpu.VMEM((hps, 1, d), F32), pltpu.VMEM((hps, 1, 1), F32)] * 2,
        compiler_params=_cparams(3),
        name="mlstm_scan",
    )(qk3, qk3, vo3, grow, gcol, qk3, qk3, vo3, grow, gcol)
    return hf.reshape(m, width), hb.reshape(m, width)


def _mlstm_post_kernel(hf_ref, hb_ref, o_ref, w_ref, y_ref, *, n_heads):
    d = hf_ref.shape[1] // n_heads
    for h in range(n_heads):
        cs = slice(h * d, (h + 1) * d)
        x = hf_ref[:, cs] + hb_ref[:, cs]
        mu = jnp.mean(x, axis=1, keepdims=True)
        xc = x - mu
        var = jnp.mean(xc * xc, axis=1, keepdims=True)
        y = xc * lax.rsqrt(var + NORM_EPS) * w_ref[:, cs]
        y_ref[:, cs] = (_sigmoid(o_ref[:, cs].astype(F32)) * y).astype(y_ref.dtype)


def _mlstm_post(hf, hb, vo, w, n_heads):
    m, width = hf.shape
    bm = _tile(m, 256)
    spec = pl.BlockSpec((bm, width), lambda i: (i, 0))
    return pl.pallas_call(
        functools.partial(_mlstm_post_kernel, n_heads=n_heads),
        grid=(m // bm,),
        in_specs=[spec, spec, pl.BlockSpec((bm, width), lambda i: (i, 1)), pl.BlockSpec((1, width), lambda i: (0, 0))],
        out_specs=spec,
        out_shape=jax.ShapeDtypeStruct((m, width), BF16),
        compiler_params=_cparams(1),
        name="mlstm_post",
    )(hf, hb, vo, w.reshape(1, width))


def _rope_tables(seq_len, scale):
    half = ROPE_DIM // 2
    inv_freq = ROPE_THETA ** (-jnp.arange(half, dtype=F32) * 2.0 / ROPE_DIM)
    ang = jnp.arange(seq_len).astype(F32)[:, None] * inv_freq[None, :]
    cos, sin = jnp.cos(ang), jnp.sin(ang)
    zeros = jnp.zeros((seq_len, half), F32)
    rest = HEAD_DIM - ROPE_DIM
    c = jnp.concatenate([cos, cos, jnp.ones((seq_len, rest), F32)], axis=1)
    s1 = jnp.concatenate([zeros, sin, jnp.zeros((seq_len, rest), F32)], axis=1)
    s2 = jnp.concatenate([-sin, zeros, jnp.zeros((seq_len, rest), F32)], axis=1)
    return [c * scale, s1 * scale, s2 * scale]


def _ffn(x, norm_g, w_gate, w_up, w_down):
    d, d_ff = w_gate.shape
    h = _rmsnorm(x, norm_g, BF16)
    (act,) = _mm([h], [(w_gate, 0, False), (w_up, 0, False)], [(0, 0), (0, 1)], [], _ep_swiglu, [BF16],
                 n=d_ff, bm=1024, bn=256, name="ffn_gate_up")
    (y,) = _mm([act], [(w_down.astype(BF16), 0, False)], [(0, 0)], [(x, "mn")],
               functools.partial(_ep_residual, 0.5), [F32], n=d, bm=512, bn=512, name="ffn_down")
    return y


def _mixers(x, batch, seq_len, mix_norm, w_in, conv_w, conv_b, gate_bias, head_norm, w_battn, w_bmlstm, w_out):
    m, d = x.shape
    attn_out = w_battn.shape[0]
    attn_w = len(ATTN_GROUPS) * attn_out
    mw = w_bmlstm.shape[0]
    n_mheads = gate_bias.shape[0] // 4
    mhd = mw // n_mheads
    n_gate = 4 * n_mheads
    offs = np.cumsum([0, attn_w, attn_w, attn_w, mw, mw, mw, mw, n_gate, d, d]).tolist()

    w_t = jnp.swapaxes(w_in, 0, 1).astype(BF16)

    h = _rmsnorm(x, mix_norm, BF16)
    (qkm,) = _mm([h], [(w_t, offs[3], True)], [(0, 0)], [], _ep_identity, [BF16], n=2 * mw, bm=1024, bn=1024,
                 name="proj_mlstm_qk")
    (vom,) = _mm([h], [(w_t, offs[5], True)], [(0, 0)], [], _ep_identity, [BF16], n=2 * mw, bm=1024, bn=1024,
                 name="proj_mlstm_vo")
    (gates,) = _mm([h], [(w_t, offs[7], True)], [(0, 0)], [], _ep_identity, [F32], n=LANES, bm=1024, bn=LANES,
                   name="proj_mlstm_gates")
    (bgate,) = _mm([h], [(w_t[offs[8]:offs[10]], 0, True)], [(0, 0)], [], _ep_sigmoid, [BF16], n=2 * d,
                   bm=1024, bn=1024, name="proj_branch_gates")

    q_tabs = _rope_tables(seq_len, HEAD_DIM ** -0.5)
    k_tabs = _rope_tables(seq_len, 1.0)
    outs, lses, dilations = [], [], []
    for g, (window, dilation) in enumerate(ATTN_GROUPS):
        c0 = g * attn_out
        qg = _attn_proj(h, w_t, offs[0] + c0, attn_out, q_tabs, batch, seq_len, dilation, f"proj_attn_q{g}")
        kg = _attn_proj(h, w_t, offs[1] + c0, attn_out, k_tabs, batch, seq_len, dilation, f"proj_attn_k{g}")
        vg = _attn_proj(h, w_t, offs[2] + c0, attn_out, [], batch, seq_len, dilation, f"proj_attn_v{g}")
        o, lse = _attn_group(qg, kg, vg, g, window // (2 * dilation))
        outs.append(o)
        lses.append(lse)
        dilations.append(dilation)
    y_attn = _attn_mix(outs, lses, dilations, seq_len)

    k_scale = jnp.concatenate([jnp.ones((mw,), F32), jnp.full((mw,), mhd ** -0.5, F32)])
    qk = _conv_silu(qkm, conv_w, conv_b, k_scale, batch, seq_len)
    gates_t = jnp.swapaxes(gates[:, :n_gate].reshape(batch, seq_len, n_gate), 1, 2)
    gp = _gate_prep(gates_t, gate_bias, n_mheads)
    grow = jnp.transpose(gp, (0, 2, 1, 3))
    gcol = jnp.transpose(gp, (0, 2, 3, 1))
    hf, hb = _mlstm_scan(qk, vom, grow, gcol, batch, seq_len, n_mheads)
    y_mlstm = _mlstm_post(hf, hb, vom, head_norm, n_mheads)

    (merged,) = _mm([y_attn, y_mlstm], [(w_battn.astype(BF16), 0, False), (w_bmlstm.astype(BF16), 0, False)],
                    [(0, 0), (1, 1)], [(bgate, ("mn", 0)), (bgate, ("mn", d))], _ep_merge, [BF16],
                    n=d, bm=1024, bn=1024, name="branch_merge")
    (y,) = _mm([merged], [(w_out.astype(BF16), 0, False)], [(0, 0)], [(x, "mn")],
               functools.partial(_ep_residual, 1.0), [F32], n=d, bm=1024, bn=1024, name="out_proj")
    return y


def kernel(x, ffn1_norm, ffn1_w_gate, ffn1_w_up, ffn1_w_down, mix_norm, w_in, mlstm_conv_w, mlstm_conv_b, mlstm_gate_bias, mlstm_head_norm, w_branch_attn, w_branch_mlstm, w_out, ffn2_norm, ffn2_w_gate, ffn2_w_up, ffn2_w_down, final_norm):
    batch, seq_len, d = x.shape
    h = x.reshape(batch * seq_len, d)
    for l in range(ffn1_norm.shape[0]):
        h = _ffn(h, ffn1_norm[l], ffn1_w_gate[l], ffn1_w_up[l], ffn1_w_down[l])
        h = _mixers(h, batch, seq_len, mix_norm[l], w_in[l], mlstm_conv_w[l], mlstm_conv_b[l], mlstm_gate_bias[l],
                    mlstm_head_norm[l], w_branch_attn[l], w_branch_mlstm[l], w_out[l])
        h = _ffn(h, ffn2_norm[l], ffn2_w_gate[l], ffn2_w_up[l], ffn2_w_down[l])
    return _rmsnorm(h, final_norm, F32).reshape(batch, seq_len, d)
```

```python
import functools

import numpy as np
import jax
import jax.numpy as jnp
from jax import lax
from jax.experimental import pallas as pl
from jax.experimental.pallas import tpu as pltpu

F32 = jnp.float32
BF16 = jnp.bfloat16

HEAD_DIM = 128
ATTN_GROUPS = ((128, 1), (512, 4), (2048, 16))
ROPE_DIM = HEAD_DIM // 4
ROPE_THETA = 500000.0
MLSTM_CHUNK = 128
MLSTM_HEADS_PER_STEP = 4
NORM_EPS = 1e-6
NEG_INF = -1e30

LANES = 128
VMEM_LIMIT_BYTES = 56 * 1024 * 1024


def _cparams(n_axes):
    return pltpu.CompilerParams(
        dimension_semantics=("arbitrary",) * n_axes, vmem_limit_bytes=VMEM_LIMIT_BYTES)


def _tile(n, pref):
    if n <= pref:
        return n
    t = (pref // LANES) * LANES
    while t >= LANES:
        if n % t == 0:
            return t
        t -= LANES
    return n


def _sigmoid(x):
    return 1.0 / (1.0 + jnp.exp(-x))


def _rmsnorm_kernel(x_ref, g_ref, o_ref):
    x = x_ref[...]
    ms = jnp.mean(x * x, axis=-1, keepdims=True)
    o_ref[...] = (x * lax.rsqrt(ms + NORM_EPS) * g_ref[...]).astype(o_ref.dtype)


def _rmsnorm(x, g, out_dtype):
    m, d = x.shape
    bm = _tile(m, 256)
    return pl.pallas_call(
        _rmsnorm_kernel,
        grid=(m // bm,),
        in_specs=[pl.BlockSpec((bm, d), lambda i: (i, 0)), pl.BlockSpec((1, d), lambda i: (0, 0))],
        out_specs=pl.BlockSpec((bm, d), lambda i: (i, 0)),
        out_shape=jax.ShapeDtypeStruct((m, d), out_dtype),
        compiler_params=_cparams(1),
        name="rmsnorm",
    )(x, g.reshape(1, d))


def _mm_kernel(*refs, n_lhs, rhs_t, pairs, n_extra, epilogue):
    lhs = refs[:n_lhs]
    n_rhs = len(rhs_t)
    rhs = refs[n_lhs:n_lhs + n_rhs]
    extras = refs[n_lhs + n_rhs:n_lhs + n_rhs + n_extra]
    outs = refs[n_lhs + n_rhs + n_extra:]
    accs = []
    for a, b in pairs:
        contract = (((1,), (1,)), ((), ())) if rhs_t[b] else (((1,), (0,)), ((), ()))
        accs.append(lax.dot_general(lhs[a][...], rhs[b][...].astype(BF16), contract, preferred_element_type=F32))
    res = epilogue(accs, [e[...] for e in extras])
    for o, r in zip(outs, res):
        o[...] = r.astype(o.dtype)


def _mm(lhs, rhs, pairs, extras, epilogue, out_dtypes, *, n, bm, bn, name, rows_inner=False):
    m = lhs[0].shape[0]
    bm = _tile(m, bm)
    bn = _tile(int(functools.reduce(np.gcd, [c0 for _, c0, _ in rhs], n)), bn)
    if rows_inner:
        grid = (n // bn, m // bm)
        ij = lambda f: (lambda j, i: f(i, j))
    else:
        grid = (m // bm, n // bn)
        ij = lambda f: f
    in_specs = [pl.BlockSpec((bm, l.shape[1]), ij(lambda i, j: (i, 0))) for l in lhs]
    for r, c0, transposed in rhs:
        if transposed:
            in_specs.append(pl.BlockSpec((bn, r.shape[1]), ij(lambda i, j, off=c0 // bn: (j + off, 0))))
        else:
            in_specs.append(pl.BlockSpec((r.shape[0], bn), ij(lambda i, j, off=c0 // bn: (0, j + off))))
    for arr, kind in extras:
        if kind == "mn" or isinstance(kind, tuple):
            off = 0 if kind == "mn" else kind[1] // bn
            in_specs.append(pl.BlockSpec((bm, bn), ij(lambda i, j, off=off: (i, j + off))))
        else:
            in_specs.append(pl.BlockSpec((1, bn), ij(lambda i, j: (0, j))))
    kern = functools.partial(
        _mm_kernel, n_lhs=len(lhs), rhs_t=tuple(t for _, _, t in rhs), pairs=tuple(pairs),
        n_extra=len(extras), epilogue=epilogue)
    return pl.pallas_call(
        kern,
        grid=grid,
        in_specs=in_specs,
        out_specs=[pl.BlockSpec((bm, bn), ij(lambda i, j: (i, j))) for _ in out_dtypes],
        out_shape=[jax.ShapeDtypeStruct((m, n), dt) for dt in out_dtypes],
        compiler_params=_cparams(2),
        name=name,
    )(*lhs, *[r for r, _, _ in rhs], *[e for e, _ in extras])


def _ep_identity(accs, extras):
    return [accs[0]]


def _ep_sigmoid(accs, extras):
    return [_sigmoid(accs[0])]


def _ep_swiglu(accs, extras):
    g, u = accs
    return [g * _sigmoid(g) * u]


def _ep_rotary(accs, extras):
    acc = accs[0]
    c, s1, s2 = extras
    pieces = []
    for h in range(acc.shape[1] // HEAD_DIM):
        x = acc[:, h * HEAD_DIM:(h + 1) * HEAD_DIM]
        half = ROPE_DIM // 2
        pieces.append(x * c + pltpu.roll(x, half, 1) * s1 + pltpu.roll(x, HEAD_DIM - half, 1) * s2)
    return [jnp.concatenate(pieces, axis=1) if len(pieces) > 1 else pieces[0]]


def _ep_merge(accs, extras):
    ya, ym = accs
    ga, gm = extras
    return [ga.astype(F32) * ya + gm.astype(F32) * ym]


def _ep_residual(scale, accs, extras):
    return [extras[0] + scale * accs[0]]


def _attn_kernel(q_ref, kp_ref, kc_ref, kn_ref, vp_ref, vc_ref, vn_ref, o_ref, lse_ref, kx_ref, vx_ref,
                 *, bs, qb, reach, seq, n_heads):
    n = pl.program_id(2)
    kx_ref[0:reach, :] = kp_ref[...]
    kx_ref[reach:reach + bs, :] = kc_ref[...]
    kx_ref[reach + bs:, :] = kn_ref[...]
    vx_ref[0:reach, :] = vp_ref[...]
    vx_ref[reach:reach + bs, :] = vc_ref[...]
    vx_ref[reach + bs:, :] = vn_ref[...]

    kw = qb + 2 * reach
    row = lax.broadcasted_iota(jnp.int32, (qb, kw), 0)
    col = lax.broadcasted_iota(jnp.int32, (qb, kw), 1)
    delta = col - row

    def body(a, carry):
        r0 = pl.multiple_of(a * qb, qb)
        kpos = n * bs + r0 - reach + col
        mask = (delta >= 0) & (delta <= 2 * reach) & (kpos >= 0) & (kpos < seq)
        for h in range(n_heads):
            cs = slice(h * HEAD_DIM, (h + 1) * HEAD_DIM)
            q = q_ref[pl.ds(r0, qb), cs]
            k = kx_ref[pl.ds(r0, kw), cs]
            v = vx_ref[pl.ds(r0, kw), cs]
            s = lax.dot_general(q, k, (((1,), (1,)), ((), ())), preferred_element_type=F32)
            s = jnp.where(mask, s, NEG_INF)
            m = jnp.max(s, axis=1, keepdims=True)
            p = jnp.exp(s - m)
            l = jnp.sum(p, axis=1, keepdims=True)
            o = jnp.dot(p.astype(BF16), v, preferred_element_type=F32)
            o_ref[h, pl.ds(r0, qb), :] = o * (1.0 / l)
            lse_ref[h, pl.ds(r0, qb), :] = jnp.broadcast_to(m + jnp.log(l), (qb, HEAD_DIM))
        return carry

    lax.fori_loop(0, bs // qb, body, 0)


def _attn_proj_kernel(*refs, rotary, dilation):
    if rotary:
        h_ref, w_ref, c_ref, s1_ref, s2_ref, o_ref = refs[:6]
    else:
        h_ref, w_ref, o_ref = refs[:3]
    acc = lax.dot_general(h_ref[...], w_ref[...], (((1,), (1,)), ((), ())), preferred_element_type=F32)
    if rotary:
        (acc,) = _ep_rotary([acc], [c_ref[...], s1_ref[...], s2_ref[...]])
    if dilation == 1:
        o_ref[...] = acc.astype(o_ref.dtype)
    else:
        scr = refs[-1]
        rows = acc.shape[0] // dilation
        for c in range(acc.shape[1] // LANES):
            scr[c] = acc[:, c * LANES:(c + 1) * LANES]
        for r in range(dilation):
            for c in range(acc.shape[1] // LANES):
                o_ref[r, :, c * LANES:(c + 1) * LANES] = scr[c, pl.ds(r, rows, stride=dilation), :].astype(o_ref.dtype)


def _attn_proj(h, w_t, c0, wg, tables, batch, seq_len, dilation, name):
    m, d_model = h.shape
    bm = _tile(seq_len, 1024)
    bn = _tile(int(np.gcd(wg, c0)), 1024)
    nbt = seq_len // bm
    assert bm % (dilation * 16) == 0 and bn % HEAD_DIM == 0
    in_specs = [pl.BlockSpec((bm, d_model), lambda i, j: (i, 0)),
                pl.BlockSpec((bn, d_model), lambda i, j: (j + c0 // bn, 0))]
    args = [h, w_t]
    for tab in tables:
        in_specs.append(pl.BlockSpec((bm, LANES), lambda i, j: (i % nbt, 0)))
        args.append(tab)
    if dilation == 1:
        out_spec = pl.BlockSpec((None, None, bm, bn), lambda i, j: (i // nbt, 0, i % nbt, j))
        scratch = []
    else:
        out_spec = pl.BlockSpec((None, dilation, bm // dilation, bn), lambda i, j: (i // nbt, 0, i % nbt, j))
        scratch = [pltpu.VMEM((bn // LANES, bm, LANES), F32)]
    return pl.pallas_call(
        functools.partial(_attn_proj_kernel, rotary=bool(tables), dilation=dilation),
        grid=(m // bm, wg // bn),
        in_specs=in_specs,
        out_specs=out_spec,
        out_shape=jax.ShapeDtypeStruct((batch, dilation, seq_len // dilation, wg), BF16),
        scratch_shapes=scratch,
        compiler_params=_cparams(2),
        name=name,
    )(*args)


def _attn_group(q, k, v, group, reach):
    batch, dilation, s_len, w = q.shape
    n_heads = w // HEAD_DIM
    qb = 2 * reach
    bs = min(512, s_len)
    assert s_len % bs == 0 and bs % qb == 0
    hb = bs // reach
    n_hblk = s_len // reach

    main = pl.BlockSpec((None, None, bs, w), lambda b, r, n: (b, r, n, 0))
    prev = pl.BlockSpec((None, None, reach, w), lambda b, r, n: (b, r, jnp.maximum(n * hb - 1, 0), 0))
    nxt = pl.BlockSpec((None, None, reach, w), lambda b, r, n: (b, r, jnp.minimum((n + 1) * hb, n_hblk - 1), 0))
    out_spec = pl.BlockSpec((None, None, n_heads, bs, HEAD_DIM), lambda b, r, n: (b, r, 0, n, 0))
    kern = functools.partial(_attn_kernel, bs=bs, qb=qb, reach=reach, seq=s_len, n_heads=n_heads)
    return pl.pallas_call(
        kern,
        grid=(batch, dilation, s_len // bs),
        in_specs=[main, prev, main, nxt, prev, main, nxt],
        out_specs=[out_spec, out_spec],
        out_shape=[jax.ShapeDtypeStruct((batch, dilation, n_heads, s_len, HEAD_DIM), F32)] * 2,
        scratch_shapes=[pltpu.VMEM((bs + 2 * reach, w), BF16), pltpu.VMEM((bs + 2 * reach, w), BF16)],
        compiler_params=_cparams(3),
        name=f"dilated_attn_g{group}",
    )(q, k, k, k, v, v, v)


def _attn_mix_kernel(*refs, dilations, bt):
    n_groups = len(dilations)
    o_refs = refs[:n_groups]
    l_refs = refs[n_groups:2 * n_groups]
    y_ref = refs[2 * n_groups]
    scr = refs[2 * n_groups + 1]
    n_heads = scr.shape[0]
    dmax = max(dilations)
    rows = bt // dmax
    for r in range(dmax):
        for h in range(n_heads):
            def rd(ref, d):
                return ref[r % d, h, pl.ds(r // d, rows, stride=dmax // d), :]
            lses = [rd(ref, d) for ref, d in zip(l_refs, dilations)]
            mx = functools.reduce(jnp.maximum, lses)
            es = [jnp.exp(l - mx) for l in lses]
            den = functools.reduce(lambda a, b: a + b, es)
            num = functools.reduce(lambda a, b: a + b, [e * rd(ref, d) for e, ref, d in zip(es, o_refs, dilations)])
            scr[h, pl.ds(r, rows, stride=dmax), :] = num / den
    for h in range(n_heads):
        y_ref[:, h * HEAD_DIM:(h + 1) * HEAD_DIM] = scr[h].astype(y_ref.dtype)


def _attn_mix(outs, lses, dilations, seq_len):
    batch, _, n_heads, _, hd = outs[0].shape
    dmax = max(dilations)
    bt = _tile(seq_len, 512)
    assert bt % (8 * dmax) == 0 and all(dmax % d == 0 for d in dilations)
    nbt = seq_len // bt
    specs = [pl.BlockSpec((None, d, n_heads, bt // d, hd), lambda b, i: (b, 0, 0, i, 0)) for d in dilations]
    return pl.pallas_call(
        functools.partial(_attn_mix_kernel, dilations=tuple(dilations), bt=bt),
        grid=(batch, nbt),
        in_specs=specs + specs,
        out_specs=pl.BlockSpec((bt, n_heads * hd), lambda b, i: (b * nbt + i, 0)),
        out_shape=jax.ShapeDtypeStruct((batch * seq_len, n_heads * hd), BF16),
        scratch_shapes=[pltpu.VMEM((n_heads, bt, hd), F32)],
        compiler_params=_cparams(2),
        name="attn_mix",
    )(*outs, *lses)


def _conv_kernel(xp_ref, xc_ref, xn_ref, w_ref, b_ref, s_ref, o_ref, xe_ref, *, bt, n_tblk, ksize, halo):
    t = pl.program_id(1)
    left = (ksize - 1) // 2
    prev = xp_ref[...].astype(F32)
    nxt = xn_ref[...].astype(F32)
    xe_ref[0:halo, :] = jnp.where(t > 0, prev, 0.0)
    xe_ref[halo:halo + bt, :] = xc_ref[...].astype(F32)
    xe_ref[halo + bt:, :] = jnp.where(t < n_tblk - 1, nxt, 0.0)
    acc = xe_ref[pl.ds(halo - left, bt), :] * w_ref[0:1, :]
    for j in range(1, ksize):
        acc = acc + xe_ref[pl.ds(halo - left + j, bt), :] * w_ref[j:j + 1, :]
    y = acc + b_ref[...]
    o_ref[...] = (y * _sigmoid(y) * s_ref[...]).astype(o_ref.dtype)


def _conv_silu(x, w, b, scale, batch, seq_len):
    m, c = x.shape
    ksize = w.shape[0]
    halo = 8
    bt = _tile(seq_len, 512)
    bc = _tile(c, 512)
    n_tblk = seq_len // bt
    hb = bt // halo
    n_hblk = seq_len // halo
    x3 = x.reshape(batch, seq_len, c)
    cur = pl.BlockSpec((None, bt, bc), lambda b_, t, j: (b_, t, j))
    prev = pl.BlockSpec((None, halo, bc), lambda b_, t, j: (b_, jnp.maximum(t * hb - 1, 0), j))
    nxt = pl.BlockSpec((None, halo, bc), lambda b_, t, j: (b_, jnp.minimum((t + 1) * hb, n_hblk - 1), j))
    wspec = pl.BlockSpec((ksize, bc), lambda b_, t, j: (0, j))
    rspec = pl.BlockSpec((1, bc), lambda b_, t, j: (0, j))
    out = pl.pallas_call(
        functools.partial(_conv_kernel, bt=bt, n_tblk=n_tblk, ksize=ksize, halo=halo),
        grid=(batch, n_tblk, c // bc),
        in_specs=[prev, cur, nxt, wspec, rspec, rspec],
        out_specs=cur,
        out_shape=jax.ShapeDtypeStruct((batch, seq_len, c), BF16),
        scratch_shapes=[pltpu.VMEM((bt + 2 * halo, bc), F32)],
        compiler_params=_cparams(3),
        name="mlstm_conv_silu",
    )(x3, x3, x3, w, b.reshape(1, c), scale.reshape(1, c))
    return out.reshape(m, c)


def _gate_prep_kernel(g_ref, b_ref, o_ref, *, n_heads, chunk):
    x = g_ref[...] + b_ref[...]
    h = n_heads
    t = x.shape[1]

    def log_sigmoid(z):
        return jnp.minimum(z, 0.0) - jnp.log(1.0 + jnp.exp(-jnp.abs(z)))

    pos = lax.broadcasted_iota(jnp.int32, (h, t), 1) & (chunk - 1)

    def chunk_cumsum(z, reverse):
        shift = 1
        while shift < chunk:
            if reverse:
                moved = pltpu.roll(z, t - shift, 1)
                keep = pos < chunk - shift
            else:
                moved = pltpu.roll(z, shift, 1)
                keep = pos >= shift
            z = z + jnp.where(keep, moved, 0.0)
            shift *= 2
        return z

    o_ref[0] = chunk_cumsum(log_sigmoid(x[h:2 * h]), False)
    o_ref[1] = x[0:h]
    o_ref[2] = chunk_cumsum(log_sigmoid(x[3 * h:4 * h]), True)
    o_ref[3] = x[2 * h:3 * h]


def _gate_prep(gates_t, bias, n_heads):
    batch, rows, t = gates_t.shape
    return pl.pallas_call(
        functools.partial(_gate_prep_kernel, n_heads=n_heads, chunk=MLSTM_CHUNK),
        grid=(batch,),
        in_specs=[pl.BlockSpec((None, rows, t), lambda b: (b, 0, 0)), pl.BlockSpec((rows, 1), lambda b: (0, 0))],
        out_specs=pl.BlockSpec((None, 4, n_heads, t), lambda b: (b, 0, 0, 0)),
        out_shape=jax.ShapeDtypeStruct((batch, 4, n_heads, t), F32),
        compiler_params=_cparams(1),
        name="mlstm_gate_prep",
    )(gates_t, bias.reshape(rows, 1))


def _mlstm_direction(q_ref, k_ref, v_ref, grow_ref, gcol_ref, h_ref, c_ref, n_ref, m_ref, *, reverse, kind):
    L = q_ref.shape[0]
    q = q_ref[...]
    k = k_ref[...]
    b_row = grow_ref[kind:kind + 1, :]
    i_row = grow_ref[kind + 1:kind + 2, :]
    b_col = gcol_ref[:, kind:kind + 1]
    i_col = gcol_ref[:, kind + 1:kind + 2]
    m_prev = m_ref[...]
    li = lax.broadcasted_iota(jnp.int32, (L, L), 0)
    si = lax.broadcasted_iota(jnp.int32, (L, L), 1)
    visible = (si >= li) if reverse else (si <= li)
    dmat = jnp.where(visible, b_col - b_row + i_row, NEG_INF)
    row_max = jnp.max(dmat, axis=1, keepdims=True)
    s_qk = lax.dot_general(q, k, (((1,), (1,)), ((), ())), preferred_element_type=F32)
    b_last = b_col[0:1, :] if reverse else b_col[L - 1:L, :]
    g = b_last - b_col + i_col
    g_max = jnp.max(g, axis=0, keepdims=True)
    yield

    inter = b_col + m_prev
    m_t = jnp.maximum(inter, row_max)
    w_inter = jnp.exp(inter - m_t)
    qk = s_qk * jnp.exp(dmat - m_t)
    m_new = jnp.maximum(b_last + m_prev, g_max)
    decay = jnp.exp(b_last + m_prev - m_new)
    kw = k.astype(F32) * jnp.exp(g - m_new)
    yield

    v = v_ref[...]
    c_prev = c_ref[...]
    n_prev = n_ref[...]
    inter_num = jnp.dot(q, c_prev.astype(BF16), preferred_element_type=F32)
    intra_num = jnp.dot(qk.astype(BF16), v, preferred_element_type=F32)
    c_add = lax.dot_general(kw.astype(BF16), v, (((0,), (0,)), ((), ())), preferred_element_type=F32)
    qn = jnp.sum(q.astype(F32) * n_prev, axis=1, keepdims=True)
    den = w_inter * qn + jnp.sum(qk, axis=1, keepdims=True)
    yield

    h_ref[...] = (w_inter * inter_num + intra_num) / jnp.maximum(jnp.abs(den), jnp.exp(-m_t))
    c_ref[...] = decay * c_prev + c_add
    n_ref[...] = decay * n_prev + jnp.sum(kw, axis=0, keepdims=True)
    m_ref[...] = m_new
    yield


def _mlstm_kernel(qf, kf, vf, growf, gcolf, qb, kb, vb, growb, gcolb, hf_ref, hb_ref,
                  cf, nf, mf, cb, nb, mb, *, heads_per_step):
    @pl.when(pl.program_id(2) == 0)
    def _():
        for r in (cf, nf, mf, cb, nb, mb):
            r[...] = jnp.zeros(r.shape, r.dtype)

    d = qf.shape[1] // heads_per_step
    chains = []
    for hh in range(heads_per_step):
        cs = slice(hh * d, (hh + 1) * d)
        chains.append(_mlstm_direction(qf.at[:, cs], kf.at[:, cs], vf.at[:, cs], growf.at[hh], gcolf.at[hh],
                                       hf_ref.at[:, cs], cf.at[hh], nf.at[hh], mf.at[hh], reverse=False, kind=0))
        chains.append(_mlstm_direction(qb.at[:, cs], kb.at[:, cs], vb.at[:, cs], growb.at[hh], gcolb.at[hh],
                                       hb_ref.at[:, cs], cb.at[hh], nb.at[hh], mb.at[hh], reverse=True, kind=2))
    for _ in range(4):
        for chain in chains:
            next(chain)


def _mlstm_scan(qk, vo, grow, gcol, batch, seq_len, n_heads):
    m, width2 = qk.shape
    width = width2 // 2
    d = width // n_heads
    L = MLSTM_CHUNK
    nc = seq_len // L
    hps = MLSTM_HEADS_PER_STEP if n_heads % MLSTM_HEADS_PER_STEP == 0 else 1
    n_hblk = n_heads // hps
    qk3 = qk.reshape(batch, seq_len, width2)
    vo3 = vo.reshape(batch, seq_len, width2)

    def seq_spec(rev, col0):
        return pl.BlockSpec((None, L, hps * d), lambda b, h, c: (b, nc - 1 - c if rev else c, h + col0))

    def row_spec(rev):
        return pl.BlockSpec((None, hps, 4, L), lambda b, h, c: (b, h, 0, nc - 1 - c if rev else c))

    def col_spec(rev):
        return pl.BlockSpec((None, hps, L, 4), lambda b, h, c: (b, h, nc - 1 - c if rev else c, 0))

    def direction(rev):
        return [seq_spec(rev, 0), seq_spec(rev, n_hblk), seq_spec(rev, 0), row_spec(rev), col_spec(rev)]

    hf, hb = pl.pallas_call(
        functools.partial(_mlstm_kernel, heads_per_step=hps),
        grid=(batch, n_hblk, nc),
        in_specs=direction(False) + direction(True),
        out_specs=[seq_spec(False, 0), seq_spec(True, 0)],
        out_shape=[jax.ShapeDtypeStruct((batch, seq_len, width), F32)] * 2,
        scratch_shapes=[pltpu.VMEM((hps, d, d), F32), pltpu.VMEM((hps, 1, d), F32), pltpu.VMEM((hps, 1, 1), F32)] * 2,
        compiler_params=_cparams(3),
        name="mlstm_scan",
    )(qk3, qk3, vo3, grow, gcol, qk3, qk3, vo3, grow, gcol)
    return hf.reshape(m, width), hb.reshape(m, width)


def _mlstm_post_kernel(hf_ref, hb_ref, o_ref, w_ref, y_ref, *, n_heads):
    d = hf_ref.shape[1] // n_heads
    for h in range(n_heads):
        cs = slice(h * d, (h + 1) * d)
        x = hf_ref[:, cs] + hb_ref[:, cs]
        mu = jnp.mean(x, axis=1, keepdims=True)
        xc = x - mu
        var = jnp.mean(xc * xc, axis=1, keepdims=True)
        y = xc * lax.rsqrt(var + NORM_EPS) * w_ref[:, cs]
        y_ref[:, cs] = (_sigmoid(o_ref[:, cs].astype(F32)) * y).astype(y_ref.dtype)


def _mlstm_post(hf, hb, vo, w, n_heads):
    m, width = hf.shape
    bm = _tile(m, 256)
    spec = pl.BlockSpec((bm, width), lambda i: (i, 0))
    return pl.pallas_call(
        functools.partial(_mlstm_post_kernel, n_heads=n_heads),
        grid=(m // bm,),
        in_specs=[spec, spec, pl.BlockSpec((bm, width), lambda i: (i, 1)), pl.BlockSpec((1, width), lambda i: (0, 0))],
        out_specs=spec,
        out_shape=jax.ShapeDtypeStruct((m, width), BF16),
        compiler_params=_cparams(1),
        name="mlstm_post",
    )(hf, hb, vo, w.reshape(1, width))


def _rope_tables(seq_len, scale):
    half = ROPE_DIM // 2
    inv_freq = ROPE_THETA ** (-jnp.arange(half, dtype=F32) * 2.0 / ROPE_DIM)
    ang = jnp.arange(seq_len).astype(F32)[:, None] * inv_freq[None, :]
    cos, sin = jnp.cos(ang), jnp.sin(ang)
    zeros = jnp.zeros((seq_len, half), F32)
    rest = HEAD_DIM - ROPE_DIM
    c = jnp.concatenate([cos, cos, jnp.ones((seq_len, rest), F32)], axis=1)
    s1 = jnp.concatenate([zeros, sin, jnp.zeros((seq_len, rest), F32)], axis=1)
    s2 = jnp.concatenate([-sin, zeros, jnp.zeros((seq_len, rest), F32)], axis=1)
    return [c * scale, s1 * scale, s2 * scale]


def _ffn(x, norm_g, w_gate, w_up, w_down):
    d, d_ff = w_gate.shape
    h = _rmsnorm(x, norm_g, BF16)
    (act,) = _mm([h], [(w_gate, 0, False), (w_up, 0, False)], [(0, 0), (0, 1)], [], _ep_swiglu, [BF16],
                 n=d_ff, bm=1024, bn=256, name="ffn_gate_up")
    (y,) = _mm([act], [(w_down.astype(BF16), 0, False)], [(0, 0)], [(x, "mn")],
               functools.partial(_ep_residual, 0.5), [F32], n=d, bm=512, bn=512, name="ffn_down", rows_inner=True)
    return y


def _mixers(x, batch, seq_len, mix_norm, w_in, conv_w, conv_b, gate_bias, head_norm, w_battn, w_bmlstm, w_out):
    m, d = x.shape
    attn_out = w_battn.shape[0]
    attn_w = len(ATTN_GROUPS) * attn_out
    mw = w_bmlstm.shape[0]
    n_mheads = gate_bias.shape[0] // 4
    mhd = mw // n_mheads
    n_gate = 4 * n_mheads
    offs = np.cumsum([0, attn_w, attn_w, attn_w, mw, mw, mw, mw, n_gate, d, d]).tolist()

    w_t = jnp.swapaxes(w_in, 0, 1).astype(BF16)

    h = _rmsnorm(x, mix_norm, BF16)
    (qkm,) = _mm([h], [(w_t, offs[3], True)], [(0, 0)], [], _ep_identity, [BF16], n=2 * mw, bm=1024, bn=1024,
                 name="proj_mlstm_qk")
    (vom,) = _mm([h], [(w_t, offs[5], True)], [(0, 0)], [], _ep_identity, [BF16], n=2 * mw, bm=1024, bn=1024,
                 name="proj_mlstm_vo")
    (gates,) = _mm([h], [(w_t, offs[7], True)], [(0, 0)], [], _ep_identity, [F32], n=LANES, bm=1024, bn=LANES,
                   name="proj_mlstm_gates")
    (bgate,) = _mm([h], [(w_t[offs[8]:offs[10]], 0, True)], [(0, 0)], [], _ep_sigmoid, [BF16], n=2 * d,
                   bm=1024, bn=1024, name="proj_branch_gates")

    q_tabs = _rope_tables(seq_len, HEAD_DIM ** -0.5)
    k_tabs = _rope_tables(seq_len, 1.0)
    outs, lses, dilations = [], [], []
    for g, (window, dilation) in enumerate(ATTN_GROUPS):
        c0 = g * attn_out
        qg = _attn_proj(h, w_t, offs[0] + c0, attn_out, q_tabs, batch, seq_len, dilation, f"proj_attn_q{g}")
        kg = _attn_proj(h, w_t, offs[1] + c0, attn_out, k_tabs, batch, seq_len, dilation, f"proj_attn_k{g}")
        vg = _attn_proj(h, w_t, offs[2] + c0, attn_out, [], batch, seq_len, dilation, f"proj_attn_v{g}")
        o, lse = _attn_group(qg, kg, vg, g, window // (2 * dilation))
        outs.append(o)
        lses.append(lse)
        dilations.append(dilation)
    y_attn = _attn_mix(outs, lses, dilations, seq_len)

    k_scale = jnp.concatenate([jnp.ones((mw,), F32), jnp.full((mw,), mhd ** -0.5, F32)])
    qk = _conv_silu(qkm, conv_w, conv_b, k_scale, batch, seq_len)
    gates_t = jnp.swapaxes(gates[:, :n_gate].reshape(batch, seq_len, n_gate), 1, 2)
    gp = _gate_prep(gates_t, gate_bias, n_mheads)
    grow = jnp.transpose(gp, (0, 2, 1, 3))
    gcol = jnp.transpose(gp, (0, 2, 3, 1))
    hf, hb = _mlstm_scan(qk, vom, grow, gcol, batch, seq_len, n_mheads)
    y_mlstm = _mlstm_post(hf, hb, vom, head_norm, n_mheads)

    (merged,) = _mm([y_attn, y_mlstm], [(w_battn.astype(BF16), 0, False), (w_bmlstm.astype(BF16), 0, False)],
                    [(0, 0), (1, 1)], [(bgate, ("mn", 0)), (bgate, ("mn", d))], _ep_merge, [BF16],
                    n=d, bm=1024, bn=1024, name="branch_merge")
    (y,) = _mm([merged], [(w_out.astype(BF16), 0, False)], [(0, 0)], [(x, "mn")],
               functools.partial(_ep_residual, 1.0), [F32], n=d, bm=1024, bn=1024, name="out_proj")
    return y


def kernel(x, ffn1_norm, ffn1_w_gate, ffn1_w_up, ffn1_w_down, mix_norm, w_in, mlstm_conv_w, mlstm_conv_b, mlstm_gate_bias, mlstm_head_norm, w_branch_attn, w_branch_mlstm, w_out, ffn2_norm, ffn2_w_gate, ffn2_w_up, ffn2_w_down, final_norm):
    batch, seq_len, d = x.shape
    h = x.reshape(batch * seq_len, d)
    for l in range(ffn1_norm.shape[0]):
        h = _ffn(h, ffn1_norm[l], ffn1_w_gate[l], ffn1_w_up[l], ffn1_w_down[l])
        h = _mixers(h, batch, seq_len, mix_norm[l], w_in[l], mlstm_conv_w[l], mlstm_conv_b[l], mlstm_gate_bias[l],
                    mlstm_head_norm[l], w_branch_attn[l], w_branch_mlstm[l], w_out[l])
        h = _ffn(h, ffn2_norm[l], ffn2_w_gate[l], ffn2_w_up[l], ffn2_w_down[l])
    return _rmsnorm(h, final_norm, F32).reshape(batch, seq_len, d)
```

```python
import functools

import numpy as np
import jax
import jax.numpy as jnp
from jax import lax
from jax.experimental import pallas as pl
from jax.experimental.pallas import tpu as pltpu

F32 = jnp.float32
BF16 = jnp.bfloat16

HEAD_DIM = 128
ATTN_GROUPS = ((128, 1), (512, 4), (2048, 16))
ROPE_DIM = HEAD_DIM // 4
ROPE_THETA = 500000.0
MLSTM_CHUNK = 128
MLSTM_HEADS_PER_STEP = 4
NORM_EPS = 1e-6
NEG_INF = -1e30

LANES = 128
VMEM_LIMIT_BYTES = 56 * 1024 * 1024


def _cparams(n_axes):
    return pltpu.CompilerParams(
        dimension_semantics=("arbitrary",) * n_axes, vmem_limit_bytes=VMEM_LIMIT_BYTES)


def _tile(n, pref):
    if n <= pref:
        return n
    t = (pref // LANES) * LANES
    while t >= LANES:
        if n % t == 0:
            return t
        t -= LANES
    return n


def _sigmoid(x):
    return 1.0 / (1.0 + jnp.exp(-x))


def _rmsnorm_kernel(x_ref, g_ref, o_ref):
    x = x_ref[...]
    ms = jnp.mean(x * x, axis=-1, keepdims=True)
    o_ref[...] = (x * lax.rsqrt(ms + NORM_EPS) * g_ref[...]).astype(o_ref.dtype)


def _rmsnorm(x, g, out_dtype):
    m, d = x.shape
    bm = _tile(m, 256)
    return pl.pallas_call(
        _rmsnorm_kernel,
        grid=(m // bm,),
        in_specs=[pl.BlockSpec((bm, d), lambda i: (i, 0)), pl.BlockSpec((1, d), lambda i: (0, 0))],
        out_specs=pl.BlockSpec((bm, d), lambda i: (i, 0)),
        out_shape=jax.ShapeDtypeStruct((m, d), out_dtype),
        compiler_params=_cparams(1),
        name="rmsnorm",
    )(x, g.reshape(1, d))


def _mm_kernel(*refs, n_lhs, rhs_t, pairs, n_extra, epilogue):
    lhs = refs[:n_lhs]
    n_rhs = len(rhs_t)
    rhs = refs[n_lhs:n_lhs + n_rhs]
    extras = refs[n_lhs + n_rhs:n_lhs + n_rhs + n_extra]
    outs = refs[n_lhs + n_rhs + n_extra:]
    accs = []
    for a, b in pairs:
        contract = (((1,), (1,)), ((), ())) if rhs_t[b] else (((1,), (0,)), ((), ()))
        accs.append(lax.dot_general(lhs[a][...], rhs[b][...].astype(BF16), contract, preferred_element_type=F32))
    res = epilogue(accs, [e[...] for e in extras])
    for o, r in zip(outs, res):
        o[...] = r.astype(o.dtype)


def _mm(lhs, rhs, pairs, extras, epilogue, out_dtypes, *, n, bm, bn, name, rows_inner=False):
    m = lhs[0].shape[0]
    bm = _tile(m, bm)
    bn = _tile(int(functools.reduce(np.gcd, [c0 for _, c0, _ in rhs], n)), bn)
    if rows_inner:
        grid = (n // bn, m // bm)
        ij = lambda f: (lambda j, i: f(i, j))
    else:
        grid = (m // bm, n // bn)
        ij = lambda f: f
    in_specs = [pl.BlockSpec((bm, l.shape[1]), ij(lambda i, j: (i, 0))) for l in lhs]
    for r, c0, transposed in rhs:
        if transposed:
            in_specs.append(pl.BlockSpec((bn, r.shape[1]), ij(lambda i, j, off=c0 // bn: (j + off, 0))))
        else:
            in_specs.append(pl.BlockSpec((r.shape[0], bn), ij(lambda i, j, off=c0 // bn: (0, j + off))))
    for arr, kind in extras:
        if kind == "mn" or isinstance(kind, tuple):
            off = 0 if kind == "mn" else kind[1] // bn
            in_specs.append(pl.BlockSpec((bm, bn), ij(lambda i, j, off=off: (i, j + off))))
        else:
            in_specs.append(pl.BlockSpec((1, bn), ij(lambda i, j: (0, j))))
    kern = functools.partial(
        _mm_kernel, n_lhs=len(lhs), rhs_t=tuple(t for _, _, t in rhs), pairs=tuple(pairs),
        n_extra=len(extras), epilogue=epilogue)
    return pl.pallas_call(
        kern,
        grid=grid,
        in_specs=in_specs,
        out_specs=[pl.BlockSpec((bm, bn), ij(lambda i, j: (i, j))) for _ in out_dtypes],
        out_shape=[jax.ShapeDtypeStruct((m, n), dt) for dt in out_dtypes],
        compiler_params=_cparams(2),
        name=name,
    )(*lhs, *[r for r, _, _ in rhs], *[e for e, _ in extras])


def _ep_identity(accs, extras):
    return [accs[0]]


def _ep_sigmoid(accs, extras):
    return [_sigmoid(accs[0])]


def _ep_swiglu(accs, extras):
    g, u = accs
    return [g * _sigmoid(g) * u]


def _ep_rotary(accs, extras):
    acc = accs[0]
    c, s1, s2 = extras
    pieces = []
    for h in range(acc.shape[1] // HEAD_DIM):
        x = acc[:, h * HEAD_DIM:(h + 1) * HEAD_DIM]
        half = ROPE_DIM // 2
        pieces.append(x * c + pltpu.roll(x, half, 1) * s1 + pltpu.roll(x, HEAD_DIM - half, 1) * s2)
    return [jnp.concatenate(pieces, axis=1) if len(pieces) > 1 else pieces[0]]


def _ep_merge(accs, extras):
    ya, ym = accs
    ga, gm = extras
    return [ga.astype(F32) * ya + gm.astype(F32) * ym]


def _ep_residual(scale, accs, extras):
    return [extras[0] + scale * accs[0]]


def _attn_kernel(q_ref, kp_ref, kc_ref, kn_ref, vp_ref, vc_ref, vn_ref, o_ref, lse_ref, kx_ref, vx_ref,
                 *, bs, qb, reach, seq, n_heads):
    n = pl.program_id(2)
    kx_ref[0:reach, :] = kp_ref[...]
    kx_ref[reach:reach + bs, :] = kc_ref[...]
    kx_ref[reach + bs:, :] = kn_ref[...]
    vx_ref[0:reach, :] = vp_ref[...]
    vx_ref[reach:reach + bs, :] = vc_ref[...]
    vx_ref[reach + bs:, :] = vn_ref[...]

    kw = qb + 2 * reach
    row = lax.broadcasted_iota(jnp.int32, (qb, kw), 0)
    col = lax.broadcasted_iota(jnp.int32, (qb, kw), 1)
    delta = col - row

    def head_chain(h, r0, mask, lse_parts):
        cs = slice(h * HEAD_DIM, (h + 1) * HEAD_DIM)
        q = q_ref[pl.ds(r0, qb), cs]
        k = kx_ref[pl.ds(r0, kw), cs]
        s = lax.dot_general(q, k, (((1,), (1,)), ((), ())), preferred_element_type=F32)
        yield
        s = jnp.where(mask, s, NEG_INF)
        m = jnp.max(s, axis=1, keepdims=True)
        yield
        p = jnp.exp(s - m)
        l = jnp.sum(p, axis=1, keepdims=True)
        v = vx_ref[pl.ds(r0, kw), cs]
        o = jnp.dot(p.astype(BF16), v, preferred_element_type=F32)
        yield
        o_ref[h, pl.ds(r0, qb), :] = o * (1.0 / l)
        lse_parts[h] = m + jnp.log(l)
        yield

    def body(a, carry):
        r0 = pl.multiple_of(a * qb, qb)
        kpos = n * bs + r0 - reach + col
        mask = (delta >= 0) & (delta <= 2 * reach) & (kpos >= 0) & (kpos < seq)
        lse_parts = [None] * n_heads
        chains = [head_chain(h, r0, mask, lse_parts) for h in range(n_heads)]
        for _ in range(4):
            for chain in chains:
                next(chain)
        owner = lax.broadcasted_iota(jnp.int32, (qb, LANES), 1) // (LANES // n_heads)
        tile = jnp.broadcast_to(lse_parts[n_heads - 1], (qb, LANES))
        for h in range(n_heads - 2, -1, -1):
            tile = jnp.where(owner == h, lse_parts[h], tile)
        lse_ref[pl.ds(r0, qb), :] = tile
        return carry

    lax.fori_loop(0, bs // qb, body, 0)


def _attn_proj_kernel(*refs, rotary, dilation):
    if rotary:
        h_ref, w_ref, c_ref, s1_ref, s2_ref, o_ref = refs[:6]
    else:
        h_ref, w_ref, o_ref = refs[:3]
    acc = lax.dot_general(h_ref[...], w_ref[...], (((1,), (1,)), ((), ())), preferred_element_type=F32)
    if rotary:
        (acc,) = _ep_rotary([acc], [c_ref[...], s1_ref[...], s2_ref[...]])
    if dilation == 1:
        o_ref[...] = acc.astype(o_ref.dtype)
    else:
        scr = refs[-1]
        rows = acc.shape[0] // dilation
        for c in range(acc.shape[1] // LANES):
            scr[c] = acc[:, c * LANES:(c + 1) * LANES]
        for r in range(dilation):
            for c in range(acc.shape[1] // LANES):
                o_ref[r, :, c * LANES:(c + 1) * LANES] = scr[c, pl.ds(r, rows, stride=dilation), :].astype(o_ref.dtype)


def _attn_proj(h, w_t, c0, wg, tables, batch, seq_len, dilation, name):
    m, d_model = h.shape
    bm = _tile(seq_len, 1024)
    bn = _tile(int(np.gcd(wg, c0)), 1024)
    nbt = seq_len // bm
    assert bm % (dilation * 16) == 0 and bn % HEAD_DIM == 0
    in_specs = [pl.BlockSpec((bm, d_model), lambda i, j: (i, 0)),
                pl.BlockSpec((bn, d_model), lambda i, j: (j + c0 // bn, 0))]
    args = [h, w_t]
    for tab in tables:
        in_specs.append(pl.BlockSpec((bm, LANES), lambda i, j: (i % nbt, 0)))
        args.append(tab)
    if dilation == 1:
        out_spec = pl.BlockSpec((None, None, bm, bn), lambda i, j: (i // nbt, 0, i % nbt, j))
        scratch = []
    else:
        out_spec = pl.BlockSpec((None, dilation, bm // dilation, bn), lambda i, j: (i // nbt, 0, i % nbt, j))
        scratch = [pltpu.VMEM((bn // LANES, bm, LANES), F32)]
    return pl.pallas_call(
        functools.partial(_attn_proj_kernel, rotary=bool(tables), dilation=dilation),
        grid=(m // bm, wg // bn),
        in_specs=in_specs,
        out_specs=out_spec,
        out_shape=jax.ShapeDtypeStruct((batch, dilation, seq_len // dilation, wg), BF16),
        scratch_shapes=scratch,
        compiler_params=_cparams(2),
        name=name,
    )(*args)


def _attn_group(q, k, v, group, reach):
    batch, dilation, s_len, w = q.shape
    n_heads = w // HEAD_DIM
    qb = 2 * reach
    bs = min(512, s_len)
    assert s_len % bs == 0 and bs % qb == 0 and LANES % n_heads == 0
    hb = bs // reach
    n_hblk = s_len // reach

    main = pl.BlockSpec((None, None, bs, w), lambda b, r, n: (b, r, n, 0))
    prev = pl.BlockSpec((None, None, reach, w), lambda b, r, n: (b, r, jnp.maximum(n * hb - 1, 0), 0))
    nxt = pl.BlockSpec((None, None, reach, w), lambda b, r, n: (b, r, jnp.minimum((n + 1) * hb, n_hblk - 1), 0))
    out_spec = pl.BlockSpec((None, None, n_heads, bs, HEAD_DIM), lambda b, r, n: (b, r, 0, n, 0))
    kern = functools.partial(_attn_kernel, bs=bs, qb=qb, reach=reach, seq=s_len, n_heads=n_heads)
    return pl.pallas_call(
        kern,
        grid=(batch, dilation, s_len // bs),
        in_specs=[main, prev, main, nxt, prev, main, nxt],
        out_specs=[out_spec, pl.BlockSpec((None, None, bs, LANES), lambda b, r, n: (b, r, n, 0))],
        out_shape=[jax.ShapeDtypeStruct((batch, dilation, n_heads, s_len, HEAD_DIM), F32),
                   jax.ShapeDtypeStruct((batch, dilation, s_len, LANES), F32)],
        scratch_shapes=[pltpu.VMEM((bs + 2 * reach, w), BF16), pltpu.VMEM((bs + 2 * reach, w), BF16)],
        compiler_params=_cparams(3),
        name=f"dilated_attn_g{group}",
    )(q, k, k, k, v, v, v)


def _attn_mix_kernel(*refs, dilations, bt):
    n_groups = len(dilations)
    o_refs = refs[:n_groups]
    l_refs = refs[n_groups:2 * n_groups]
    y_ref = refs[2 * n_groups]
    scr = refs[2 * n_groups + 1]
    n_heads = scr.shape[0]
    dmax = max(dilations)
    rows = bt // dmax
    for r in range(dmax):
        lse_tiles = [ref[r % d, pl.ds(r // d, rows, stride=dmax // d), :] for ref, d in zip(l_refs, dilations)]
        for h in range(n_heads):
            lane = h * (LANES // n_heads)
            lses = [t[:, lane:lane + 1] for t in lse_tiles]
            mx = functools.reduce(jnp.maximum, lses)
            es = [jnp.exp(l - mx) for l in lses]
            inv = 1.0 / functools.reduce(lambda a, b: a + b, es)
            outs = [ref[r % d, h, pl.ds(r // d, rows, stride=dmax // d), :] for ref, d in zip(o_refs, dilations)]
            num = functools.reduce(lambda a, b: a + b, [(e * inv) * o for e, o in zip(es, outs)])
            scr[h, pl.ds(r, rows, stride=dmax), :] = num
    for h in range(n_heads):
        y_ref[:, h * HEAD_DIM:(h + 1) * HEAD_DIM] = scr[h].astype(y_ref.dtype)


def _attn_mix(outs, lses, dilations, seq_len):
    batch, _, n_heads, _, hd = outs[0].shape
    dmax = max(dilations)
    bt = _tile(seq_len, 512)
    assert bt % (8 * dmax) == 0 and all(dmax % d == 0 for d in dilations)
    nbt = seq_len // bt
    specs = [pl.BlockSpec((None, d, n_heads, bt // d, hd), lambda b, i: (b, 0, 0, i, 0)) for d in dilations]
    lse_specs = [pl.BlockSpec((None, d, bt // d, LANES), lambda b, i: (b, 0, i, 0)) for d in dilations]
    return pl.pallas_call(
        functools.partial(_attn_mix_kernel, dilations=tuple(dilations), bt=bt),
        grid=(batch, nbt),
        in_specs=specs + lse_specs,
        out_specs=pl.BlockSpec((bt, n_heads * hd), lambda b, i: (b * nbt + i, 0)),
        out_shape=jax.ShapeDtypeStruct((batch * seq_len, n_heads * hd), BF16),
        scratch_shapes=[pltpu.VMEM((n_heads, bt, hd), F32)],
        compiler_params=_cparams(2),
        name="attn_mix",
    )(*outs, *lses)


def _conv_kernel(xp_ref, xc_ref, xn_ref, w_ref, b_ref, s_ref, o_ref, xe_ref, *, bt, n_tblk, ksize, halo):
    t = pl.program_id(1)
    left = (ksize - 1) // 2
    prev = xp_ref[...].astype(F32)
    nxt = xn_ref[...].astype(F32)
    xe_ref[0:halo, :] = jnp.where(t > 0, prev, 0.0)
    xe_ref[halo:halo + bt, :] = xc_ref[...].astype(F32)
    xe_ref[halo + bt:, :] = jnp.where(t < n_tblk - 1, nxt, 0.0)
    acc = xe_ref[pl.ds(halo - left, bt), :] * w_ref[0:1, :]
    for j in range(1, ksize):
        acc = acc + xe_ref[pl.ds(halo - left + j, bt), :] * w_ref[j:j + 1, :]
    y = acc + b_ref[...]
    o_ref[...] = (y * _sigmoid(y) * s_ref[...]).astype(o_ref.dtype)


def _conv_silu(x, w, b, scale, batch, seq_len):
    m, c = x.shape
    ksize = w.shape[0]
    halo = 8
    bt = _tile(seq_len, 512)
    bc = _tile(c, 512)
    n_tblk = seq_len // bt
    hb = bt // halo
    n_hblk = seq_len // halo
    x3 = x.reshape(batch, seq_len, c)
    cur = pl.BlockSpec((None, bt, bc), lambda b_, t, j: (b_, t, j))
    prev = pl.BlockSpec((None, halo, bc), lambda b_, t, j: (b_, jnp.maximum(t * hb - 1, 0), j))
    nxt = pl.BlockSpec((None, halo, bc), lambda b_, t, j: (b_, jnp.minimum((t + 1) * hb, n_hblk - 1), j))
    wspec = pl.BlockSpec((ksize, bc), lambda b_, t, j: (0, j))
    rspec = pl.BlockSpec((1, bc), lambda b_, t, j: (0, j))
    out = pl.pallas_call(
        functools.partial(_conv_kernel, bt=bt, n_tblk=n_tblk, ksize=ksize, halo=halo),
        grid=(batch, n_tblk, c // bc),
        in_specs=[prev, cur, nxt, wspec, rspec, rspec],
        out_specs=cur,
        out_shape=jax.ShapeDtypeStruct((batch, seq_len, c), BF16),
        scratch_shapes=[pltpu.VMEM((bt + 2 * halo, bc), F32)],
        compiler_params=_cparams(3),
        name="mlstm_conv_silu",
    )(x3, x3, x3, w, b.reshape(1, c), scale.reshape(1, c))
    return out.reshape(m, c)


def _gate_prep_kernel(g_ref, b_ref, o_ref, *, n_heads, chunk):
    x = g_ref[...] + b_ref[...]
    h = n_heads
    t = x.shape[1]

    def log_sigmoid(z):
        return jnp.minimum(z, 0.0) - jnp.log(1.0 + jnp.exp(-jnp.abs(z)))

    pos = lax.broadcasted_iota(jnp.int32, (h, t), 1) & (chunk - 1)

    def chunk_cumsum(z, reverse):
        shift = 1
        while shift < chunk:
            if reverse:
                moved = pltpu.roll(z, t - shift, 1)
                keep = pos < chunk - shift
            else:
                moved = pltpu.roll(z, shift, 1)
                keep = pos >= shift
            z = z + jnp.where(keep, moved, 0.0)
            shift *= 2
        return z

    o_ref[0] = chunk_cumsum(log_sigmoid(x[h:2 * h]), False)
    o_ref[1] = x[0:h]
    o_ref[2] = chunk_cumsum(log_sigmoid(x[3 * h:4 * h]), True)
    o_ref[3] = x[2 * h:3 * h]


def _gate_prep(gates_t, bias, n_heads):
    batch, rows, t = gates_t.shape
    return pl.pallas_call(
        functools.partial(_gate_prep_kernel, n_heads=n_heads, chunk=MLSTM_CHUNK),
        grid=(batch,),
        in_specs=[pl.BlockSpec((None, rows, t), lambda b: (b, 0, 0)), pl.BlockSpec((rows, 1), lambda b: (0, 0))],
        out_specs=pl.BlockSpec((None, 4, n_heads, t), lambda b: (b, 0, 0, 0)),
        out_shape=jax.ShapeDtypeStruct((batch, 4, n_heads, t), F32),
        compiler_params=_cparams(1),
        name="mlstm_gate_prep",
    )(gates_t, bias.reshape(rows, 1))


def _mlstm_direction(q_ref, k_ref, v_ref, grow_ref, gcol_ref, h_ref, c_ref, n_ref, m_ref, *, reverse, kind):
    L = q_ref.shape[0]
    q = q_ref[...]
    k = k_ref[...]
    b_row = grow_ref[kind:kind + 1, :]
    i_row = grow_ref[kind + 1:kind + 2, :]
    b_col = gcol_ref[:, kind:kind + 1]
    i_col = gcol_ref[:, kind + 1:kind + 2]
    m_prev = m_ref[...]
    li = lax.broadcasted_iota(jnp.int32, (L, L), 0)
    si = lax.broadcasted_iota(jnp.int32, (L, L), 1)
    visible = (si >= li) if reverse else (si <= li)
    dmat = jnp.where(visible, b_col - b_row + i_row, NEG_INF)
    row_max = jnp.max(dmat, axis=1, keepdims=True)
    s_qk = lax.dot_general(q, k, (((1,), (1,)), ((), ())), preferred_element_type=F32)
    b_last = b_col[0:1, :] if reverse else b_col[L - 1:L, :]
    g = b_last - b_col + i_col
    g_max = jnp.max(g, axis=0, keepdims=True)
    yield

    inter = b_col + m_prev
    m_t = jnp.maximum(inter, row_max)
    w_inter = jnp.exp(inter - m_t)
    qk = s_qk * jnp.exp(dmat - m_t)
    m_new = jnp.maximum(b_last + m_prev, g_max)
    decay = jnp.exp(b_last + m_prev - m_new)
    kw = k.astype(F32) * jnp.exp(g - m_new)
    yield

    v = v_ref[...]
    c_prev = c_ref[...]
    n_prev = n_ref[...]
    inter_num = jnp.dot(q, c_prev.astype(BF16), preferred_element_type=F32)
    intra_num = jnp.dot(qk.astype(BF16), v, preferred_element_type=F32)
    c_add = lax.dot_general(kw.astype(BF16), v, (((0,), (0,)), ((), ())), preferred_element_type=F32)
    qn = jnp.sum(q.astype(F32) * n_prev, axis=1, keepdims=True)
    den = w_inter * qn + jnp.sum(qk, axis=1, keepdims=True)
    yield

    h_ref[...] = (w_inter * inter_num + intra_num) / jnp.maximum(jnp.abs(den), jnp.exp(-m_t))
    c_ref[...] = decay * c_prev + c_add
    n_ref[...] = decay * n_prev + jnp.sum(kw, axis=0, keepdims=True)
    m_ref[...] = m_new
    yield


def _mlstm_kernel(qf, kf, vf, growf, gcolf, qb, kb, vb, growb, gcolb, hf_ref, hb_ref,
                  cf, nf, mf, cb, nb, mb, *, heads_per_step):
    @pl.when(pl.program_id(2) == 0)
    def _():
        for r in (cf, nf, mf, cb, nb, mb):
            r[...] = jnp.zeros(r.shape, r.dtype)

    d = qf.shape[1] // heads_per_step
    chains = []
    for hh in range(heads_per_step):
        cs = slice(hh * d, (hh + 1) * d)
        chains.append(_mlstm_direction(qf.at[:, cs], kf.at[:, cs], vf.at[:, cs], growf.at[hh], gcolf.at[hh],
                                       hf_ref.at[:, cs], cf.at[hh], nf.at[hh], mf.at[hh], reverse=False, kind=0))
        chains.append(_mlstm_direction(qb.at[:, cs], kb.at[:, cs], vb.at[:, cs], growb.at[hh], gcolb.at[hh],
                                       hb_ref.at[:, cs], cb.at[hh], nb.at[hh], mb.at[hh], reverse=True, kind=2))
    for _ in range(4):
        for chain in chains:
            next(chain)


def _mlstm_scan(qk, vo, grow, gcol, batch, seq_len, n_heads):
    m, width2 = qk.shape
    width = width2 // 2
    d = width // n_heads
    L = MLSTM_CHUNK
    nc = seq_len // L
    hps = MLSTM_HEADS_PER_STEP if n_heads % MLSTM_HEADS_PER_STEP == 0 else 1
    n_hblk = n_heads // hps
    qk3 = qk.reshape(batch, seq_len, width2)
    vo3 = vo.reshape(batch, seq_len, width2)

    def seq_spec(rev, col0):
        return pl.BlockSpec((None, L, hps * d), lambda b, h, c: (b, nc - 1 - c if rev else c, h + col0))

    def row_spec(rev):
        return pl.BlockSpec((None, hps, 4, L), lambda b, h, c: (b, h, 0, nc - 1 - c if rev else c))

    def col_spec(rev):
        return pl.BlockSpec((None, hps, L, 4), lambda b, h, c: (b, h, nc - 1 - c if rev else c, 0))

    def direction(rev):
        return [seq_spec(rev, 0), seq_spec(rev, n_hblk), seq_spec(rev, 0), row_spec(rev), col_spec(rev)]

    hf, hb = pl.pallas_call(
        functools.partial(_mlstm_kernel, heads_per_step=hps),
        grid=(batch, n_hblk, nc),
        in_specs=direction(False) + direction(True),
        out_specs=[seq_spec(False, 0), seq_spec(True, 0)],
        out_shape=[jax.ShapeDtypeStruct((batch, seq_len, width), F32)] * 2,
        scratch_shapes=[pltpu.VMEM((hps, d, d), F32), pltpu.VMEM((hps, 1, d), F32), pltpu.VMEM((hps, 1, 1), F32)] * 2,
        compiler_params=_cparams(3),
        name="mlstm_scan",
    )(qk3, qk3, vo3, grow, gcol, qk3, qk3, vo3, grow, gcol)
    return hf.reshape(m, width), hb.reshape(m, width)


def _mlstm_post_kernel(hf_ref, hb_ref, o_ref, w_ref, y_ref, *, n_heads):
    d = hf_ref.shape[1] // n_heads
    for h in range(n_heads):
        cs = slice(h * d, (h + 1) * d)
        x = hf_ref[:, cs] + hb_ref[:, cs]
        mu = jnp.mean(x, axis=1, keepdims=True)
        xc = x - mu
        var = jnp.mean(xc * xc, axis=1, keepdims=True)
        y = xc * lax.rsqrt(var + NORM_EPS) * w_ref[:, cs]
        y_ref[:, cs] = (_sigmoid(o_ref[:, cs].astype(F32)) * y).astype(y_ref.dtype)


def _mlstm_post(hf, hb, vo, w, n_heads):
    m, width = hf.shape
    bm = _tile(m, 256)
    spec = pl.BlockSpec((bm, width), lambda i: (i, 0))
    return pl.pallas_call(
        functools.partial(_mlstm_post_kernel, n_heads=n_heads),
        grid=(m // bm,),
        in_specs=[spec, spec, pl.BlockSpec((bm, width), lambda i: (i, 1)), pl.BlockSpec((1, width), lambda i: (0, 0))],
        out_specs=spec,
        out_shape=jax.ShapeDtypeStruct((m, width), BF16),
        compiler_params=_cparams(1),
        name="mlstm_post",
    )(hf, hb, vo, w.reshape(1, width))


def _rope_tables(seq_len, scale):
    half = ROPE_DIM // 2
    inv_freq = ROPE_THETA ** (-jnp.arange(half, dtype=F32) * 2.0 / ROPE_DIM)
    ang = jnp.arange(seq_len).astype(F32)[:, None] * inv_freq[None, :]
    cos, sin = jnp.cos(ang), jnp.sin(ang)
    zeros = jnp.zeros((seq_len, half), F32)
    rest = HEAD_DIM - ROPE_DIM
    c = jnp.concatenate([cos, cos, jnp.ones((seq_len, rest), F32)], axis=1)
    s1 = jnp.concatenate([zeros, sin, jnp.zeros((seq_len, rest), F32)], axis=1)
    s2 = jnp.concatenate([-sin, zeros, jnp.zeros((seq_len, rest), F32)], axis=1)
    return [c * scale, s1 * scale, s2 * scale]


def _ffn(x, norm_g, w_gate, w_up, w_down):
    d, d_ff = w_gate.shape
    h = _rmsnorm(x, norm_g, BF16)
    (act,) = _mm([h], [(w_gate, 0, False), (w_up, 0, False)], [(0, 0), (0, 1)], [], _ep_swiglu, [BF16],
                 n=d_ff, bm=2048, bn=256, name="ffn_gate_up")
    (y,) = _mm([act], [(w_down.astype(BF16), 0, False)], [(0, 0)], [(x, "mn")],
               functools.partial(_ep_residual, 0.5), [F32], n=d, bm=512, bn=512, name="ffn_down", rows_inner=True)
    return y


def _mixers(x, batch, seq_len, mix_norm, w_in, conv_w, conv_b, gate_bias, head_norm, w_battn, w_bmlstm, w_out):
    m, d = x.shape
    attn_out = w_battn.shape[0]
    attn_w = len(ATTN_GROUPS) * attn_out
    mw = w_bmlstm.shape[0]
    n_mheads = gate_bias.shape[0] // 4
    mhd = mw // n_mheads
    n_gate = 4 * n_mheads
    offs = np.cumsum([0, attn_w, attn_w, attn_w, mw, mw, mw, mw, n_gate, d, d]).tolist()

    w_t = jnp.swapaxes(w_in, 0, 1).astype(BF16)

    h = _rmsnorm(x, mix_norm, BF16)
    (qkm,) = _mm([h], [(w_t, offs[3], True)], [(0, 0)], [], _ep_identity, [BF16], n=2 * mw, bm=1024, bn=1024,
                 name="proj_mlstm_qk")
    (vom,) = _mm([h], [(w_t, offs[5], True)], [(0, 0)], [], _ep_identity, [BF16], n=2 * mw, bm=1024, bn=1024,
                 name="proj_mlstm_vo")
    (gates,) = _mm([h], [(w_t, offs[7], True)], [(0, 0)], [], _ep_identity, [F32], n=LANES, bm=1024, bn=LANES,
                   name="proj_mlstm_gates")
    (bgate,) = _mm([h], [(w_t[offs[8]:offs[10]], 0, True)], [(0, 0)], [], _ep_sigmoid, [BF16], n=2 * d,
                   bm=1024, bn=1024, name="proj_branch_gates")

    q_tabs = _rope_tables(seq_len, HEAD_DIM ** -0.5)
    k_tabs = _rope_tables(seq_len, 1.0)
    outs, lses, dilations = [], [], []
    for g, (window, dilation) in enumerate(ATTN_GROUPS):
        c0 = g * attn_out
        qg = _attn_proj(h, w_t, offs[0] + c0, attn_out, q_tabs, batch, seq_len, dilation, f"proj_attn_q{g}")
        kg = _attn_proj(h, w_t, offs[1] + c0, attn_out, k_tabs, batch, seq_len, dilation, f"proj_attn_k{g}")
        vg = _attn_proj(h, w_t, offs[2] + c0, attn_out, [], batch, seq_len, dilation, f"proj_attn_v{g}")
        o, lse = _attn_group(qg, kg, vg, g, window // (2 * dilation))
        outs.append(o)
        lses.append(lse)
        dilations.append(dilation)
    y_attn = _attn_mix(outs, lses, dilations, seq_len)

    k_scale = jnp.concatenate([jnp.ones((mw,), F32), jnp.full((mw,), mhd ** -0.5, F32)])
    qk = _conv_silu(qkm, conv_w, conv_b, k_scale, batch, seq_len)
    gates_t = jnp.swapaxes(gates[:, :n_gate].reshape(batch, seq_len, n_gate), 1, 2)
    gp = _gate_prep(gates_t, gate_bias, n_mheads)
    grow = jnp.transpose(gp, (0, 2, 1, 3))
    gcol = jnp.transpose(gp, (0, 2, 3, 1))
    hf, hb = _mlstm_scan(qk, vom, grow, gcol, batch, seq_len, n_mheads)
    y_mlstm = _mlstm_post(hf, hb, vom, head_norm, n_mheads)

    (merged,) = _mm([y_attn, y_mlstm], [(w_battn.astype(BF16), 0, False), (w_bmlstm.astype(BF16), 0, False)],
                    [(0, 0), (1, 1)], [(bgate, ("mn", 0)), (bgate, ("mn", d))], _ep_merge, [BF16],
                    n=d, bm=1024, bn=1024, name="branch_merge")
    (y,) = _mm([merged], [(w_out.astype(BF16), 0, False)], [(0, 0)], [(x, "mn")],
               functools.partial(_ep_residual, 1.0), [F32], n=d, bm=1024, bn=1024, name="out_proj")
    return y


def kernel(x, ffn1_norm, ffn1_w_gate, ffn1_w_up, ffn1_w_down, mix_norm, w_in, mlstm_conv_w, mlstm_conv_b, mlstm_gate_bias, mlstm_head_norm, w_branch_attn, w_branch_mlstm, w_out, ffn2_norm, ffn2_w_gate, ffn2_w_up, ffn2_w_down, final_norm):
    batch, seq_len, d = x.shape
    h = x.reshape(batch * seq_len, d)
    for l in range(ffn1_norm.shape[0]):
        h = _ffn(h, ffn1_norm[l], ffn1_w_gate[l], ffn1_w_up[l], ffn1_w_down[l])
        h = _mixers(h, batch, seq_len, mix_norm[l], w_in[l], mlstm_conv_w[l], mlstm_conv_b[l], mlstm_gate_bias[l],
                    mlstm_head_norm[l], w_branch_attn[l], w_branch_mlstm[l], w_out[l])
        h = _ffn(h, ffn2_norm[l], ffn2_w_gate[l], ffn2_w_up[l], ffn2_w_down[l])
    return _rmsnorm(h, final_norm, F32).reshape(batch, seq_len, d)
```

```python
import functools

import numpy as np
import jax
import jax.numpy as jnp
from jax import lax
from jax.experimental import pallas as pl
from jax.experimental.pallas import tpu as pltpu

F32 = jnp.float32
BF16 = jnp.bfloat16

HEAD_DIM = 128
ATTN_GROUPS = ((128, 1), (512, 4), (2048, 16))
ROPE_DIM = HEAD_DIM // 4
ROPE_THETA = 500000.0
MLSTM_CHUNK = 128
MLSTM_HEADS_PER_STEP = 4
NORM_EPS = 1e-6
NEG_INF = -1e30

LANES = 128
BF16_SUBLANE_ROWS = 16
VMEM_LIMIT_BYTES = 56 * 1024 * 1024


def _cparams(n_axes):
    return pltpu.CompilerParams(
        dimension_semantics=("arbitrary",) * n_axes, vmem_limit_bytes=VMEM_LIMIT_BYTES)


def _tile(n, pref):
    if n <= pref:
        return n
    t = (pref // LANES) * LANES
    while t >= LANES:
        if n % t == 0:
            return t
        t -= LANES
    return n


def _sigmoid(x):
    return 0.5 * jnp.tanh(0.5 * x) + 0.5


def _rmsnorm_kernel(x_ref, g_ref, o_ref):
    x = x_ref[...]
    ms = jnp.mean(x * x, axis=-1, keepdims=True)
    o_ref[...] = (x * lax.rsqrt(ms + NORM_EPS) * g_ref[...]).astype(o_ref.dtype)


def _rmsnorm(x, g, out_dtype):
    m, d = x.shape
    bm = _tile(m, 256)
    return pl.pallas_call(
        _rmsnorm_kernel,
        grid=(m // bm,),
        in_specs=[pl.BlockSpec((bm, d), lambda i: (i, 0)), pl.BlockSpec((1, d), lambda i: (0, 0))],
        out_specs=pl.BlockSpec((bm, d), lambda i: (i, 0)),
        out_shape=jax.ShapeDtypeStruct((m, d), out_dtype),
        compiler_params=_cparams(1),
        name="rmsnorm",
    )(x, g.reshape(1, d))


def _mm_kernel(*refs, n_lhs, rhs_t, pairs, n_extra, epilogue):
    lhs = refs[:n_lhs]
    n_rhs = len(rhs_t)
    rhs = refs[n_lhs:n_lhs + n_rhs]
    extras = refs[n_lhs + n_rhs:n_lhs + n_rhs + n_extra]
    outs = refs[n_lhs + n_rhs + n_extra:]
    accs = []
    for a, b in pairs:
        contract = (((1,), (1,)), ((), ())) if rhs_t[b] else (((1,), (0,)), ((), ()))
        accs.append(lax.dot_general(lhs[a][...], rhs[b][...].astype(BF16), contract, preferred_element_type=F32))
    res = epilogue(accs, [e[...] for e in extras])
    for o, r in zip(outs, res):
        o[...] = r.astype(o.dtype)


def _mm(lhs, rhs, pairs, extras, epilogue, out_dtypes, *, n, bm, bn, name, rows_inner=False):
    m = lhs[0].shape[0]
    bm = _tile(m, bm)
    bn = _tile(int(functools.reduce(np.gcd, [c0 for _, c0, t in rhs if not t], n)), bn)
    if rows_inner:
        grid = (n // bn, m // bm)
        ij = lambda f: (lambda j, i: f(i, j))
    else:
        grid = (m // bm, n // bn)
        ij = lambda f: f
    in_specs = [pl.BlockSpec((bm, l.shape[1]), ij(lambda i, j: (i, 0))) for l in lhs]
    for r, c0, transposed in rhs:
        if transposed:
            assert c0 % BF16_SUBLANE_ROWS == 0
            in_specs.append(pl.BlockSpec(
                (pl.Element(bn), pl.Element(r.shape[1])),
                ij(lambda i, j, c0=c0: (pl.multiple_of(j * bn + c0, BF16_SUBLANE_ROWS), 0))))
        else:
            in_specs.append(pl.BlockSpec((r.shape[0], bn), ij(lambda i, j, off=c0 // bn: (0, j + off))))
    for arr, kind in extras:
        if kind == "mn" or isinstance(kind, tuple):
            off = 0 if kind == "mn" else kind[1] // bn
            in_specs.append(pl.BlockSpec((bm, bn), ij(lambda i, j, off=off: (i, j + off))))
        else:
            in_specs.append(pl.BlockSpec((1, bn), ij(lambda i, j: (0, j))))
    kern = functools.partial(
        _mm_kernel, n_lhs=len(lhs), rhs_t=tuple(t for _, _, t in rhs), pairs=tuple(pairs),
        n_extra=len(extras), epilogue=epilogue)
    return pl.pallas_call(
        kern,
        grid=grid,
        in_specs=in_specs,
        out_specs=[pl.BlockSpec((bm, bn), ij(lambda i, j: (i, j))) for _ in out_dtypes],
        out_shape=[jax.ShapeDtypeStruct((m, n), dt) for dt in out_dtypes],
        compiler_params=_cparams(2),
        name=name,
    )(*lhs, *[r for r, _, _ in rhs], *[e for e, _ in extras])


def _ep_identity(accs, extras):
    return [accs[0]]


def _ep_sigmoid(accs, extras):
    return [_sigmoid(accs[0])]


def _ep_swiglu(accs, extras):
    g, u = accs
    return [g * _sigmoid(g) * u]


def _ep_rotary(accs, extras):
    acc = accs[0]
    c, s1, s2 = extras
    pieces = []
    for h in range(acc.shape[1] // HEAD_DIM):
        x = acc[:, h * HEAD_DIM:(h + 1) * HEAD_DIM]
        half = ROPE_DIM // 2
        pieces.append(x * c + pltpu.roll(x, half, 1) * s1 + pltpu.roll(x, HEAD_DIM - half, 1) * s2)
    return [jnp.concatenate(pieces, axis=1) if len(pieces) > 1 else pieces[0]]


def _ep_merge(accs, extras):
    ya, ym = accs
    ga, gm = extras
    return [ga.astype(F32) * ya + gm.astype(F32) * ym]


def _ep_residual(scale, accs, extras):
    return [extras[0] + scale * accs[0]]


def _attn_kernel(q_ref, kp_ref, kc_ref, kn_ref, vp_ref, vc_ref, vn_ref, o_ref, lse_ref, kx_ref, vx_ref,
                 *, bs, qb, reach, seq, n_heads):
    n = pl.program_id(2)
    kx_ref[0:reach, :] = kp_ref[...]
    kx_ref[reach:reach + bs, :] = kc_ref[...]
    kx_ref[reach + bs:, :] = kn_ref[...]
    vx_ref[0:reach, :] = vp_ref[...]
    vx_ref[reach:reach + bs, :] = vc_ref[...]
    vx_ref[reach + bs:, :] = vn_ref[...]

    kw = qb + 2 * reach
    row = lax.broadcasted_iota(jnp.int32, (qb, kw), 0)
    col = lax.broadcasted_iota(jnp.int32, (qb, kw), 1)
    delta = col - row

    def head_chain(h, r0, mask, lse_parts):
        cs = slice(h * HEAD_DIM, (h + 1) * HEAD_DIM)
        q = q_ref[pl.ds(r0, qb), cs]
        k = kx_ref[pl.ds(r0, kw), cs]
        s = lax.dot_general(q, k, (((1,), (1,)), ((), ())), preferred_element_type=F32)
        yield
        s = jnp.where(mask, s, NEG_INF)
        m = jnp.max(s, axis=1, keepdims=True)
        yield
        p = jnp.exp(s - m)
        l = jnp.sum(p, axis=1, keepdims=True)
        v = vx_ref[pl.ds(r0, kw), cs]
        o = jnp.dot(p.astype(BF16), v, preferred_element_type=F32)
        yield
        o_ref[h, pl.ds(r0, qb), :] = o * (1.0 / l)
        lse_parts[h] = m + jnp.log(l)
        yield

    def body(a, carry):
        r0 = pl.multiple_of(a * qb, qb)
        kpos = n * bs + r0 - reach + col
        mask = (delta >= 0) & (delta <= 2 * reach) & (kpos >= 0) & (kpos < seq)
        lse_parts = [None] * n_heads
        chains = [head_chain(h, r0, mask, lse_parts) for h in range(n_heads)]
        for _ in range(4):
            for chain in chains:
                next(chain)
        owner = lax.broadcasted_iota(jnp.int32, (qb, LANES), 1) // (LANES // n_heads)
        tile = jnp.broadcast_to(lse_parts[n_heads - 1], (qb, LANES))
        for h in range(n_heads - 2, -1, -1):
            tile = jnp.where(owner == h, lse_parts[h], tile)
        lse_ref[pl.ds(r0, qb), :] = tile
        return carry

    lax.fori_loop(0, bs // qb, body, 0)


def _attn_proj_kernel(*refs, rotary, dilation):
    if rotary:
        h_ref, w_ref, c_ref, s1_ref, s2_ref, o_ref = refs[:6]
    else:
        h_ref, w_ref, o_ref = refs[:3]
    acc = lax.dot_general(h_ref[...], w_ref[...], (((1,), (1,)), ((), ())), preferred_element_type=F32)
    if rotary:
        (acc,) = _ep_rotary([acc], [c_ref[...], s1_ref[...], s2_ref[...]])
    if dilation == 1:
        o_ref[...] = acc.astype(o_ref.dtype)
    else:
        scr = refs[-1]
        rows = acc.shape[0] // dilation
        for c in range(acc.shape[1] // LANES):
            scr[c] = acc[:, c * LANES:(c + 1) * LANES]
        for r in range(dilation):
            for c in range(acc.shape[1] // LANES):
                o_ref[r, :, c * LANES:(c + 1) * LANES] = scr[c, pl.ds(r, rows, stride=dilation), :].astype(o_ref.dtype)


def _attn_proj(h, w_t, c0, wg, tables, batch, seq_len, dilation, name):
    m, d_model = h.shape
    bm = _tile(seq_len, 1024)
    bn = _tile(int(np.gcd(wg, c0)), 1024)
    nbt = seq_len // bm
    assert bm % (dilation * 16) == 0 and bn % HEAD_DIM == 0
    in_specs = [pl.BlockSpec((bm, d_model), lambda i, j: (i, 0)),
                pl.BlockSpec((bn, d_model), lambda i, j: (j + c0 // bn, 0))]
    args = [h, w_t]
    for tab in tables:
        in_specs.append(pl.BlockSpec((bm, LANES), lambda i, j: (i % nbt, 0)))
        args.append(tab)
    if dilation == 1:
        out_spec = pl.BlockSpec((None, None, bm, bn), lambda i, j: (i // nbt, 0, i % nbt, j))
        scratch = []
    else:
        out_spec = pl.BlockSpec((None, dilation, bm // dilation, bn), lambda i, j: (i // nbt, 0, i % nbt, j))
        scratch = [pltpu.VMEM((bn // LANES, bm, LANES), F32)]
    return pl.pallas_call(
        functools.partial(_attn_proj_kernel, rotary=bool(tables), dilation=dilation),
        grid=(m // bm, wg // bn),
        in_specs=in_specs,
        out_specs=out_spec,
        out_shape=jax.ShapeDtypeStruct((batch, dilation, seq_len // dilation, wg), BF16),
        scratch_shapes=scratch,
        compiler_params=_cparams(2),
        name=name,
    )(*args)


def _attn_group(q, k, v, group, reach):
    batch, dilation, s_len, w = q.shape
    n_heads = w // HEAD_DIM
    qb = 2 * reach
    bs = min(512, s_len)
    assert s_len % bs == 0 and bs % qb == 0 and LANES % n_heads == 0
    hb = bs // reach
    n_hblk = s_len // reach

    main = pl.BlockSpec((None, None, bs, w), lambda b, r, n: (b, r, n, 0))
    prev = pl.BlockSpec((None, None, reach, w), lambda b, r, n: (b, r, jnp.maximum(n * hb - 1, 0), 0))
    nxt = pl.BlockSpec((None, None, reach, w), lambda b, r, n: (b, r, jnp.minimum((n + 1) * hb, n_hblk - 1), 0))
    out_spec = pl.BlockSpec((None, None, n_heads, bs, HEAD_DIM), lambda b, r, n: (b, r, 0, n, 0))
    kern = functools.partial(_attn_kernel, bs=bs, qb=qb, reach=reach, seq=s_len, n_heads=n_heads)
    return pl.pallas_call(
        kern,
        grid=(batch, dilation, s_len // bs),
        in_specs=[main, prev, main, nxt, prev, main, nxt],
        out_specs=[out_spec, pl.BlockSpec((None, None, bs, LANES), lambda b, r, n: (b, r, n, 0))],
        out_shape=[jax.ShapeDtypeStruct((batch, dilation, n_heads, s_len, HEAD_DIM), F32),
                   jax.ShapeDtypeStruct((batch, dilation, s_len, LANES), F32)],
        scratch_shapes=[pltpu.VMEM((bs + 2 * reach, w), BF16), pltpu.VMEM((bs + 2 * reach, w), BF16)],
        compiler_params=_cparams(3),
        name=f"dilated_attn_g{group}",
    )(q, k, k, k, v, v, v)


def _attn_mix_kernel(*refs, dilations, bt):
    n_groups = len(dilations)
    o_refs = refs[:n_groups]
    l_refs = refs[n_groups:2 * n_groups]
    y_ref = refs[2 * n_groups]
    scr = refs[2 * n_groups + 1]
    n_heads = scr.shape[0]
    dmax = max(dilations)
    rows = bt // dmax
    for r in range(dmax):
        lse_tiles = [ref[r % d, pl.ds(r // d, rows, stride=dmax // d), :] for ref, d in zip(l_refs, dilations)]
        for h in range(n_heads):
            lane = h * (LANES // n_heads)
            lses = [t[:, lane:lane + 1] for t in lse_tiles]
            mx = functools.reduce(jnp.maximum, lses)
            es = [jnp.exp(l - mx) for l in lses]
            inv = 1.0 / functools.reduce(lambda a, b: a + b, es)
            outs = [ref[r % d, h, pl.ds(r // d, rows, stride=dmax // d), :] for ref, d in zip(o_refs, dilations)]
            num = functools.reduce(lambda a, b: a + b, [(e * inv) * o for e, o in zip(es, outs)])
            scr[h, pl.ds(r, rows, stride=dmax), :] = num
    for h in range(n_heads):
        y_ref[:, h * HEAD_DIM:(h + 1) * HEAD_DIM] = scr[h].astype(y_ref.dtype)


def _attn_mix(outs, lses, dilations, seq_len):
    batch, _, n_heads, _, hd = outs[0].shape
    dmax = max(dilations)
    bt = _tile(seq_len, 512)
    assert bt % (8 * dmax) == 0 and all(dmax % d == 0 for d in dilations)
    nbt = seq_len // bt
    specs = [pl.BlockSpec((None, d, n_heads, bt // d, hd), lambda b, i: (b, 0, 0, i, 0)) for d in dilations]
    lse_specs = [pl.BlockSpec((None, d, bt // d, LANES), lambda b, i: (b, 0, i, 0)) for d in dilations]
    return pl.pallas_call(
        functools.partial(_attn_mix_kernel, dilations=tuple(dilations), bt=bt),
        grid=(batch, nbt),
        in_specs=specs + lse_specs,
        out_specs=pl.BlockSpec((bt, n_heads * hd), lambda b, i: (b * nbt + i, 0)),
        out_shape=jax.ShapeDtypeStruct((batch * seq_len, n_heads * hd), BF16),
        scratch_shapes=[pltpu.VMEM((n_heads, bt, hd), F32)],
        compiler_params=_cparams(2),
        name="attn_mix",
    )(*outs, *lses)


def _conv_kernel(xp_ref, xc_ref, xn_ref, w_ref, b_ref, s_ref, o_ref, xe_ref, *, bt, n_tblk, ksize, halo):
    t = pl.program_id(1)
    left = (ksize - 1) // 2
    prev = xp_ref[...].astype(F32)
    nxt = xn_ref[...].astype(F32)
    xe_ref[0:halo, :] = jnp.where(t > 0, prev, 0.0)
    xe_ref[halo:halo + bt, :] = xc_ref[...].astype(F32)
    xe_ref[halo + bt:, :] = jnp.where(t < n_tblk - 1, nxt, 0.0)
    acc = xe_ref[pl.ds(halo - left, bt), :] * w_ref[0:1, :]
    for j in range(1, ksize):
        acc = acc + xe_ref[pl.ds(halo - left + j, bt), :] * w_ref[j:j + 1, :]
    y = acc + b_ref[...]
    o_ref[...] = (y * _sigmoid(y) * s_ref[...]).astype(o_ref.dtype)


def _conv_silu(x, w, b, scale, batch, seq_len):
    m, c = x.shape
    ksize = w.shape[0]
    halo = 8
    bt = _tile(seq_len, 512)
    bc = _tile(c, 512)
    n_tblk = seq_len // bt
    hb = bt // halo
    n_hblk = seq_len // halo
    x3 = x.reshape(batch, seq_len, c)
    cur = pl.BlockSpec((None, bt, bc), lambda b_, t, j: (b_, t, j))
    prev = pl.BlockSpec((None, halo, bc), lambda b_, t, j: (b_, jnp.maximum(t * hb - 1, 0), j))
    nxt = pl.BlockSpec((None, halo, bc), lambda b_, t, j: (b_, jnp.minimum((t + 1) * hb, n_hblk - 1), j))
    wspec = pl.BlockSpec((ksize, bc), lambda b_, t, j: (0, j))
    rspec = pl.BlockSpec((1, bc), lambda b_, t, j: (0, j))
    out = pl.pallas_call(
        functools.partial(_conv_kernel, bt=bt, n_tblk=n_tblk, ksize=ksize, halo=halo),
        grid=(batch, n_tblk, c // bc),
        in_specs=[prev, cur, nxt, wspec, rspec, rspec],
        out_specs=cur,
        out_shape=jax.ShapeDtypeStruct((batch, seq_len, c), BF16),
        scratch_shapes=[pltpu.VMEM((bt + 2 * halo, bc), F32)],
        compiler_params=_cparams(3),
        name="mlstm_conv_silu",
    )(x3, x3, x3, w, b.reshape(1, c), scale.reshape(1, c))
    return out.reshape(m, c)


def _gate_prep_kernel(g_ref, b_ref, o_ref, *, n_heads, chunk):
    x = g_ref[...] + b_ref[...]
    h = n_heads
    t = x.shape[1]

    def log_sigmoid(z):
        return jnp.minimum(z, 0.0) - jnp.log(1.0 + jnp.exp(-jnp.abs(z)))

    pos = lax.broadcasted_iota(jnp.int32, (h, t), 1) & (chunk - 1)

    def chunk_cumsum(z, reverse):
        shift = 1
        while shift < chunk:
            if reverse:
                moved = pltpu.roll(z, t - shift, 1)
                keep = pos < chunk - shift
            else:
                moved = pltpu.roll(z, shift, 1)
                keep = pos >= shift
            z = z + jnp.where(keep, moved, 0.0)
            shift *= 2
        return z

    o_ref[0] = chunk_cumsum(log_sigmoid(x[h:2 * h]), False)
    o_ref[1] = x[0:h]
    o_ref[2] = chunk_cumsum(log_sigmoid(x[3 * h:4 * h]), True)
    o_ref[3] = x[2 * h:3 * h]


def _gate_prep(gates_t, bias, n_heads, batch):
    rows, m = gates_t.shape
    t = m // batch
    return pl.pallas_call(
        functools.partial(_gate_prep_kernel, n_heads=n_heads, chunk=MLSTM_CHUNK),
        grid=(batch,),
        in_specs=[pl.BlockSpec((rows, t), lambda b: (0, b)), pl.BlockSpec((rows, 1), lambda b: (0, 0))],
        out_specs=pl.BlockSpec((None, 4, n_heads, t), lambda b: (b, 0, 0, 0)),
        out_shape=jax.ShapeDtypeStruct((batch, 4, n_heads, t), F32),
        compiler_params=_cparams(1),
        name="mlstm_gate_prep",
    )(gates_t, bias.reshape(rows, 1))


def _mlstm_direction(q_ref, k_ref, v_ref, grow_ref, gcol_ref, h_ref, c_ref, n_ref, m_ref, *, reverse, kind):
    L = q_ref.shape[0]
    q = q_ref[...]
    k = k_ref[...]
    b_row = grow_ref[kind:kind + 1, :]
    i_row = grow_ref[kind + 1:kind + 2, :]
    b_col = gcol_ref[:, kind:kind + 1]
    i_col = gcol_ref[:, kind + 1:kind + 2]
    m_prev = m_ref[...]
    li = lax.broadcasted_iota(jnp.int32, (L, L), 0)
    si = lax.broadcasted_iota(jnp.int32, (L, L), 1)
    visible = (si >= li) if reverse else (si <= li)
    dmat = jnp.where(visible, b_col - b_row + i_row, NEG_INF)
    row_max = jnp.max(dmat, axis=1, keepdims=True)
    s_qk = lax.dot_general(q, k, (((1,), (1,)), ((), ())), preferred_element_type=F32)
    b_last = b_col[0:1, :] if reverse else b_col[L - 1:L, :]
    g = b_last - b_col + i_col
    g_max = jnp.max(g, axis=0, keepdims=True)
    yield

    inter = b_col + m_prev
    m_t = jnp.maximum(inter, row_max)
    w_inter = jnp.exp(inter - m_t)
    qk = s_qk * jnp.exp(dmat - m_t)
    m_new = jnp.maximum(b_last + m_prev, g_max)
    decay = jnp.exp(b_last + m_prev - m_new)
    kw = k.astype(F32) * jnp.exp(g - m_new)
    yield

    v = v_ref[...]
    c_prev = c_ref[...]
    n_prev = n_ref[...]
    inter_num = jnp.dot(q, c_prev.astype(BF16), preferred_element_type=F32)
    intra_num = jnp.dot(qk.astype(BF16), v, preferred_element_type=F32)
    c_add = lax.dot_general(kw.astype(BF16), v, (((0,), (0,)), ((), ())), preferred_element_type=F32)
    qn = jnp.sum(q.astype(F32) * n_prev, axis=1, keepdims=True)
    den = w_inter * qn + jnp.sum(qk, axis=1, keepdims=True)
    yield

    h_ref[...] = (w_inter * inter_num + intra_num) / jnp.maximum(jnp.abs(den), jnp.exp(-m_t))
    c_ref[...] = decay * c_prev + c_add
    n_ref[...] = decay * n_prev + jnp.sum(kw, axis=0, keepdims=True)
    m_ref[...] = m_new
    yield


def _mlstm_kernel(qf, kf, vf, growf, gcolf, qb, kb, vb, growb, gcolb, hf_ref, hb_ref,
                  cf, nf, mf, cb, nb, mb, *, heads_per_step):
    @pl.when(pl.program_id(2) == 0)
    def _():
        for r in (cf, nf, mf, cb, nb, mb):
            r[...] = jnp.zeros(r.shape, r.dtype)

    d = qf.shape[1] // heads_per_step
    chains = []
    for hh in range(heads_per_step):
        cs = slice(hh * d, (hh + 1) * d)
        chains.append(_mlstm_direction(qf.at[:, cs], kf.at[:, cs], vf.at[:, cs], growf.at[hh], gcolf.at[hh],
                                       hf_ref.at[:, cs], cf.at[hh], nf.at[hh], mf.at[hh], reverse=False, kind=0))
        chains.append(_mlstm_direction(qb.at[:, cs], kb.at[:, cs], vb.at[:, cs], growb.at[hh], gcolb.at[hh],
                                       hb_ref.at[:, cs], cb.at[hh], nb.at[hh], mb.at[hh], reverse=True, kind=2))
    for _ in range(4):
        for chain in chains:
            next(chain)


def _mlstm_scan(qk, vo, grow, gcol, batch, seq_len, n_heads):
    m, width2 = qk.shape
    width = width2 // 2
    d = width // n_heads
    L = MLSTM_CHUNK
    nc = seq_len // L
    hps = MLSTM_HEADS_PER_STEP if n_heads % MLSTM_HEADS_PER_STEP == 0 else 1
    n_hblk = n_heads // hps
    qk3 = qk.reshape(batch, seq_len, width2)
    vo3 = vo.reshape(batch, seq_len, width2)

    def seq_spec(rev, col0):
        return pl.BlockSpec((None, L, hps * d), lambda b, h, c: (b, nc - 1 - c if rev else c, h + col0))

    def row_spec(rev):
        return pl.BlockSpec((None, hps, 4, L), lambda b, h, c: (b, h, 0, nc - 1 - c if rev else c))

    def col_spec(rev):
        return pl.BlockSpec((None, hps, L, 4), lambda b, h, c: (b, h, nc - 1 - c if rev else c, 0))

    def direction(rev):
        return [seq_spec(rev, 0), seq_spec(rev, n_hblk), seq_spec(rev, 0), row_spec(rev), col_spec(rev)]

    hf, hb = pl.pallas_call(
        functools.partial(_mlstm_kernel, heads_per_step=hps),
        grid=(batch, n_hblk, nc),
        in_specs=direction(False) + direction(True),
        out_specs=[seq_spec(False, 0), seq_spec(True, 0)],
        out_shape=[jax.ShapeDtypeStruct((batch, seq_len, width), F32)] * 2,
        scratch_shapes=[pltpu.VMEM((hps, d, d), F32), pltpu.VMEM((hps, 1, d), F32), pltpu.VMEM((hps, 1, 1), F32)] * 2,
        compiler_params=_cparams(3),
        name="mlstm_scan",
    )(qk3, qk3, vo3, grow, gcol, qk3, qk3, vo3, grow, gcol)
    return hf.reshape(m, width), hb.reshape(m, width)


def _mlstm_post_kernel(hf_ref, hb_ref, o_ref, w_ref, y_ref, *, n_heads):
    d = hf_ref.shape[1] // n_heads
    for h in range(n_heads):
        cs = slice(h * d, (h + 1) * d)
        x = hf_ref[:, cs] + hb_ref[:, cs]
        mu = jnp.mean(x, axis=1, keepdims=True)
        xc = x - mu
        var = jnp.mean(xc * xc, axis=1, keepdims=True)
        y = xc * lax.rsqrt(var + NORM_EPS) * w_ref[:, cs]
        y_ref[:, cs] = (_sigmoid(o_ref[:, cs].astype(F32)) * y).astype(y_ref.dtype)


def _mlstm_post(hf, hb, vo, w, n_heads):
    m, width = hf.shape
    bm = _tile(m, 256)
    spec = pl.BlockSpec((bm, width), lambda i: (i, 0))
    return pl.pallas_call(
        functools.partial(_mlstm_post_kernel, n_heads=n_heads),
        grid=(m // bm,),
        in_specs=[spec, spec, pl.BlockSpec((bm, width), lambda i: (i, 1)), pl.BlockSpec((1, width), lambda i: (0, 0))],
        out_specs=spec,
        out_shape=jax.ShapeDtypeStruct((m, width), BF16),
        compiler_params=_cparams(1),
        name="mlstm_post",
    )(hf, hb, vo, w.reshape(1, width))


def _rope_tables(seq_len, scale):
    half = ROPE_DIM // 2
    inv_freq = ROPE_THETA ** (-jnp.arange(half, dtype=F32) * 2.0 / ROPE_DIM)
    ang = jnp.arange(seq_len).astype(F32)[:, None] * inv_freq[None, :]
    cos, sin = jnp.cos(ang), jnp.sin(ang)
    zeros = jnp.zeros((seq_len, half), F32)
    rest = HEAD_DIM - ROPE_DIM
    c = jnp.concatenate([cos, cos, jnp.ones((seq_len, rest), F32)], axis=1)
    s1 = jnp.concatenate([zeros, sin, jnp.zeros((seq_len, rest), F32)], axis=1)
    s2 = jnp.concatenate([-sin, zeros, jnp.zeros((seq_len, rest), F32)], axis=1)
    return [c * scale, s1 * scale, s2 * scale]


def _ffn(x, norm_g, w_gate, w_up, w_down):
    d, d_ff = w_gate.shape
    h = _rmsnorm(x, norm_g, BF16)
    (act,) = _mm([h], [(w_gate, 0, False), (w_up, 0, False)], [(0, 0), (0, 1)], [], _ep_swiglu, [BF16],
                 n=d_ff, bm=2048, bn=256, name="ffn_gate_up")
    (y,) = _mm([act], [(w_down.astype(BF16), 0, False)], [(0, 0)], [(x, "mn")],
               functools.partial(_ep_residual, 0.5), [F32], n=d, bm=512, bn=512, name="ffn_down", rows_inner=True)
    return y


def _mixers(x, batch, seq_len, mix_norm, w_in, conv_w, conv_b, gate_bias, head_norm, w_battn, w_bmlstm, w_out):
    m, d = x.shape
    attn_out = w_battn.shape[0]
    attn_w = len(ATTN_GROUPS) * attn_out
    mw = w_bmlstm.shape[0]
    n_mheads = gate_bias.shape[0] // 4
    mhd = mw // n_mheads
    n_gate = 4 * n_mheads
    offs = np.cumsum([0, attn_w, attn_w, attn_w, mw, mw, mw, mw, n_gate, d, d]).tolist()

    w_t = jnp.swapaxes(w_in, 0, 1).astype(BF16)

    h = _rmsnorm(x, mix_norm, BF16)
    (qkm,) = _mm([h], [(w_t, offs[3], True)], [(0, 0)], [], _ep_identity, [BF16], n=2 * mw, bm=1024, bn=1024,
                 name="proj_mlstm_qk")
    (vom,) = _mm([h], [(w_t, offs[5], True)], [(0, 0)], [], _ep_identity, [BF16], n=2 * mw, bm=1024, bn=1024,
                 name="proj_mlstm_vo")
    (gates_t,) = _mm([w_t[offs[7]:offs[8]]], [(h, 0, True)], [(0, 0)], [], _ep_identity, [F32], n=m, bm=n_gate,
                     bn=1024, name="proj_mlstm_gates")
    (bgate,) = _mm([h], [(w_t, offs[8], True)], [(0, 0)], [], _ep_sigmoid, [BF16], n=2 * d,
                   bm=1024, bn=1024, name="proj_branch_gates")

    q_tabs = _rope_tables(seq_len, HEAD_DIM ** -0.5)
    k_tabs = _rope_tables(seq_len, 1.0)
    outs, lses, dilations = [], [], []
    for g, (window, dilation) in enumerate(ATTN_GROUPS):
        c0 = g * attn_out
        qg = _attn_proj(h, w_t, offs[0] + c0, attn_out, q_tabs, batch, seq_len, dilation, f"proj_attn_q{g}")
        kg = _attn_proj(h, w_t, offs[1] + c0, attn_out, k_tabs, batch, seq_len, dilation, f"proj_attn_k{g}")
        vg = _attn_proj(h, w_t, offs[2] + c0, attn_out, [], batch, seq_len, dilation, f"proj_attn_v{g}")
        o, lse = _attn_group(qg, kg, vg, g, window // (2 * dilation))
        outs.append(o)
        lses.append(lse)
        dilations.append(dilation)
    y_attn = _attn_mix(outs, lses, dilations, seq_len)

    k_scale = jnp.concatenate([jnp.ones((mw,), F32), jnp.full((mw,), mhd ** -0.5, F32)])
    qk = _conv_silu(qkm, conv_w, conv_b, k_scale, batch, seq_len)
    gp = _gate_prep(gates_t, gate_bias, n_mheads, batch)
    grow = jnp.transpose(gp, (0, 2, 1, 3))
    gcol = jnp.transpose(gp, (0, 2, 3, 1))
    hf, hb = _mlstm_scan(qk, vom, grow, gcol, batch, seq_len, n_mheads)
    y_mlstm = _mlstm_post(hf, hb, vom, head_norm, n_mheads)

    (merged,) = _mm([y_attn, y_mlstm], [(w_battn.astype(BF16), 0, False), (w_bmlstm.astype(BF16), 0, False)],
                    [(0, 0), (1, 1)], [(bgate, ("mn", 0)), (bgate, ("mn", d))], _ep_merge, [BF16],
                    n=d, bm=1024, bn=1024, name="branch_merge")
    (y,) = _mm([merged], [(w_out.astype(BF16), 0, False)], [(0, 0)], [(x, "mn")],
               functools.partial(_ep_residual, 1.0), [F32], n=d, bm=1024, bn=1024, name="out_proj")
    return y


def kernel(x, ffn1_norm, ffn1_w_gate, ffn1_w_up, ffn1_w_down, mix_norm, w_in, mlstm_conv_w, mlstm_conv_b, mlstm_gate_bias, mlstm_head_norm, w_branch_attn, w_branch_mlstm, w_out, ffn2_norm, ffn2_w_gate, ffn2_w_up, ffn2_w_down, final_norm):
    batch, seq_len, d = x.shape
    h = x.reshape(batch * seq_len, d)
    for l in range(ffn1_norm.shape[0]):
        h = _ffn(h, ffn1_norm[l], ffn1_w_gate[l], ffn1_w_up[l], ffn1_w_down[l])
        h = _mixers(h, batch, seq_len, mix_norm[l], w_in[l], mlstm_conv_w[l], mlstm_conv_b[l], mlstm_gate_bias[l],
                    mlstm_head_norm[l], w_branch_attn[l], w_branch_mlstm[l], w_out[l])
        h = _ffn(h, ffn2_norm[l], ffn2_w_gate[l], ffn2_w_up[l], ffn2_w_down[l])
    return _rmsnorm(h, final_norm, F32).reshape(batch, seq_len, d)
```

```python
import functools

import numpy as np
import jax
import jax.numpy as jnp
from jax import lax
from jax.experimental import pallas as pl
from jax.experimental.pallas import tpu as pltpu

F32 = jnp.float32
BF16 = jnp.bfloat16

HEAD_DIM = 128
ATTN_GROUPS = ((128, 1), (512, 4), (2048, 16))
ROPE_DIM = HEAD_DIM // 4
ROPE_THETA = 500000.0
MLSTM_CHUNK = 128
MLSTM_HEADS_PER_STEP = 8
NORM_EPS = 1e-6
NEG_INF = -1e30

LANES = 128
BF16_SUBLANE_ROWS = 16
VMEM_LIMIT_BYTES = 56 * 1024 * 1024


def _cparams(n_axes):
    return pltpu.CompilerParams(
        dimension_semantics=("arbitrary",) * n_axes, vmem_limit_bytes=VMEM_LIMIT_BYTES)


def _tile(n, pref):
    if n <= pref:
        return n
    t = (pref // LANES) * LANES
    while t >= LANES:
        if n % t == 0:
            return t
        t -= LANES
    return n


def _sigmoid(x):
    return 0.5 * jnp.tanh(0.5 * x) + 0.5


def _rmsnorm_kernel(x_ref, g_ref, o_ref):
    x = x_ref[...]
    ms = jnp.mean(x * x, axis=-1, keepdims=True)
    o_ref[...] = (x * lax.rsqrt(ms + NORM_EPS) * g_ref[...]).astype(o_ref.dtype)


def _rmsnorm(x, g, out_dtype):
    m, d = x.shape
    bm = _tile(m, 256)
    return pl.pallas_call(
        _rmsnorm_kernel,
        grid=(m // bm,),
        in_specs=[pl.BlockSpec((bm, d), lambda i: (i, 0)), pl.BlockSpec((1, d), lambda i: (0, 0))],
        out_specs=pl.BlockSpec((bm, d), lambda i: (i, 0)),
        out_shape=jax.ShapeDtypeStruct((m, d), out_dtype),
        compiler_params=_cparams(1),
        name="rmsnorm",
    )(x, g.reshape(1, d))


def _mm_kernel(*refs, n_lhs, rhs_t, pairs, n_extra, epilogue):
    lhs = refs[:n_lhs]
    n_rhs = len(rhs_t)
    rhs = refs[n_lhs:n_lhs + n_rhs]
    extras = refs[n_lhs + n_rhs:n_lhs + n_rhs + n_extra]
    outs = refs[n_lhs + n_rhs + n_extra:]
    accs = []
    for a, b in pairs:
        contract = (((1,), (1,)), ((), ())) if rhs_t[b] else (((1,), (0,)), ((), ()))
        accs.append(lax.dot_general(lhs[a][...], rhs[b][...].astype(BF16), contract, preferred_element_type=F32))
    res = epilogue(accs, [e[...] for e in extras])
    for o, r in zip(outs, res):
        o[...] = r.astype(o.dtype)


def _mm(lhs, rhs, pairs, extras, epilogue, out_dtypes, *, n, bm, bn, name, rows_inner=False):
    m = lhs[0].shape[0]
    bm = _tile(m, bm)
    bn = _tile(int(functools.reduce(np.gcd, [c0 for _, c0, t in rhs if not t], n)), bn)
    if rows_inner:
        grid = (n // bn, m // bm)
        ij = lambda f: (lambda j, i: f(i, j))
    else:
        grid = (m // bm, n // bn)
        ij = lambda f: f
    in_specs = [pl.BlockSpec((bm, l.shape[1]), ij(lambda i, j: (i, 0))) for l in lhs]
    for r, c0, transposed in rhs:
        if transposed:
            assert c0 % BF16_SUBLANE_ROWS == 0
            in_specs.append(pl.BlockSpec(
                (pl.Element(bn), pl.Element(r.shape[1])),
                ij(lambda i, j, c0=c0: (pl.multiple_of(j * bn + c0, BF16_SUBLANE_ROWS), 0))))
        else:
            in_specs.append(pl.BlockSpec((r.shape[0], bn), ij(lambda i, j, off=c0 // bn: (0, j + off))))
    for arr, kind in extras:
        if kind == "mn" or isinstance(kind, tuple):
            off = 0 if kind == "mn" else kind[1] // bn
            in_specs.append(pl.BlockSpec((bm, bn), ij(lambda i, j, off=off: (i, j + off))))
        else:
            in_specs.append(pl.BlockSpec((1, bn), ij(lambda i, j: (0, j))))
    kern = functools.partial(
        _mm_kernel, n_lhs=len(lhs), rhs_t=tuple(t for _, _, t in rhs), pairs=tuple(pairs),
        n_extra=len(extras), epilogue=epilogue)
    return pl.pallas_call(
        kern,
        grid=grid,
        in_specs=in_specs,
        out_specs=[pl.BlockSpec((bm, bn), ij(lambda i, j: (i, j))) for _ in out_dtypes],
        out_shape=[jax.ShapeDtypeStruct((m, n), dt) for dt in out_dtypes],
        compiler_params=_cparams(2),
        name=name,
    )(*lhs, *[r for r, _, _ in rhs], *[e for e, _ in extras])


def _ep_identity(accs, extras):
    return [accs[0]]


def _ep_sigmoid(accs, extras):
    return [_sigmoid(accs[0])]


def _ep_swiglu(accs, extras):
    g, u = accs
    return [g * _sigmoid(g) * u]


def _ep_rotary(accs, extras):
    acc = accs[0]
    c, s1, s2 = extras
    pieces = []
    for h in range(acc.shape[1] // HEAD_DIM):
        x = acc[:, h * HEAD_DIM:(h + 1) * HEAD_DIM]
        half = ROPE_DIM // 2
        pieces.append(x * c + pltpu.roll(x, half, 1) * s1 + pltpu.roll(x, HEAD_DIM - half, 1) * s2)
    return [jnp.concatenate(pieces, axis=1) if len(pieces) > 1 else pieces[0]]


def _ep_merge(accs, extras):
    ya, ym = accs
    ga, gm = extras
    return [ga.astype(F32) * ya + gm.astype(F32) * ym]


def _ep_residual(scale, accs, extras):
    return [extras[0] + scale * accs[0]]


def _attn_kernel(q_ref, kp_ref, kc_ref, kn_ref, vp_ref, vc_ref, vn_ref, o_ref, lse_ref, kx_ref, vx_ref,
                 *, bs, qb, reach, seq, n_heads):
    n = pl.program_id(2)
    kx_ref[0:reach, :] = kp_ref[...]
    kx_ref[reach:reach + bs, :] = kc_ref[...]
    kx_ref[reach + bs:, :] = kn_ref[...]
    vx_ref[0:reach, :] = vp_ref[...]
    vx_ref[reach:reach + bs, :] = vc_ref[...]
    vx_ref[reach + bs:, :] = vn_ref[...]

    kw = qb + 2 * reach
    row = lax.broadcasted_iota(jnp.int32, (qb, kw), 0)
    col = lax.broadcasted_iota(jnp.int32, (qb, kw), 1)
    delta = col - row

    def head_chain(h, r0, mask, lse_parts):
        cs = slice(h * HEAD_DIM, (h + 1) * HEAD_DIM)
        q = q_ref[pl.ds(r0, qb), cs]
        k = kx_ref[pl.ds(r0, kw), cs]
        s = lax.dot_general(q, k, (((1,), (1,)), ((), ())), preferred_element_type=F32)
        yield
        s = jnp.where(mask, s, NEG_INF)
        m = jnp.max(s, axis=1, keepdims=True)
        yield
        p = jnp.exp(s - m)
        l = jnp.sum(p, axis=1, keepdims=True)
        v = vx_ref[pl.ds(r0, kw), cs]
        o = jnp.dot(p.astype(BF16), v, preferred_element_type=F32)
        yield
        o_ref[h, pl.ds(r0, qb), :] = o * (1.0 / l)
        lse_parts[h] = m + jnp.log(l)
        yield

    def body(a, carry):
        r0 = pl.multiple_of(a * qb, qb)
        kpos = n * bs + r0 - reach + col
        mask = (delta >= 0) & (delta <= 2 * reach) & (kpos >= 0) & (kpos < seq)
        lse_parts = [None] * n_heads
        chains = [head_chain(h, r0, mask, lse_parts) for h in range(n_heads)]
        for _ in range(4):
            for chain in chains:
                next(chain)
        owner = lax.broadcasted_iota(jnp.int32, (qb, LANES), 1) // (LANES // n_heads)
        tile = jnp.broadcast_to(lse_parts[n_heads - 1], (qb, LANES))
        for h in range(n_heads - 2, -1, -1):
            tile = jnp.where(owner == h, lse_parts[h], tile)
        lse_ref[pl.ds(r0, qb), :] = tile
        return carry

    lax.fori_loop(0, bs // qb, body, 0)


def _attn_proj_kernel(*refs, rotary, dilation):
    if rotary:
        h_ref, w_ref, c_ref, s1_ref, s2_ref, o_ref = refs[:6]
    else:
        h_ref, w_ref, o_ref = refs[:3]
    acc = lax.dot_general(h_ref[...], w_ref[...], (((1,), (1,)), ((), ())), preferred_element_type=F32)
    if rotary:
        (acc,) = _ep_rotary([acc], [c_ref[...], s1_ref[...], s2_ref[...]])
    if dilation == 1:
        o_ref[...] = acc.astype(o_ref.dtype)
    else:
        scr = refs[-1]
        rows = acc.shape[0] // dilation
        for c in range(acc.shape[1] // LANES):
            scr[c] = acc[:, c * LANES:(c + 1) * LANES]
        for r in range(dilation):
            for c in range(acc.shape[1] // LANES):
                o_ref[r, :, c * LANES:(c + 1) * LANES] = scr[c, pl.ds(r, rows, stride=dilation), :].astype(o_ref.dtype)


def _attn_proj(h, w_t, c0, wg, tables, batch, seq_len, dilation, name):
    m, d_model = h.shape
    bm = _tile(seq_len, 1024)
    bn = _tile(int(np.gcd(wg, c0)), 1024)
    nbt = seq_len // bm
    assert bm % (dilation * 16) == 0 and bn % HEAD_DIM == 0
    in_specs = [pl.BlockSpec((bm, d_model), lambda i, j: (i, 0)),
                pl.BlockSpec((bn, d_model), lambda i, j: (j + c0 // bn, 0))]
    args = [h, w_t]
    for tab in tables:
        in_specs.append(pl.BlockSpec((bm, LANES), lambda i, j: (i % nbt, 0)))
        args.append(tab)
    if dilation == 1:
        out_spec = pl.BlockSpec((None, None, bm, bn), lambda i, j: (i // nbt, 0, i % nbt, j))
        scratch = []
    else:
        out_spec = pl.BlockSpec((None, dilation, bm // dilation, bn), lambda i, j: (i // nbt, 0, i % nbt, j))
        scratch = [pltpu.VMEM((bn // LANES, bm, LANES), F32)]
    return pl.pallas_call(
        functools.partial(_attn_proj_kernel, rotary=bool(tables), dilation=dilation),
        grid=(m // bm, wg // bn),
        in_specs=in_specs,
        out_specs=out_spec,
        out_shape=jax.ShapeDtypeStruct((batch, dilation, seq_len // dilation, wg), BF16),
        scratch_shapes=scratch,
        compiler_params=_cparams(2),
        name=name,
    )(*args)


def _attn_group(q, k, v, group, reach):
    batch, dilation, s_len, w = q.shape
    n_heads = w // HEAD_DIM
    qb = 2 * reach
    bs = min(512, s_len)
    assert s_len % bs == 0 and bs % qb == 0 and LANES % n_heads == 0
    hb = bs // reach
    n_hblk = s_len // reach

    main = pl.BlockSpec((None, None, bs, w), lambda b, r, n: (b, r, n, 0))
    prev = pl.BlockSpec((None, None, reach, w), lambda b, r, n: (b, r, jnp.maximum(n * hb - 1, 0), 0))
    nxt = pl.BlockSpec((None, None, reach, w), lambda b, r, n: (b, r, jnp.minimum((n + 1) * hb, n_hblk - 1), 0))
    out_spec = pl.BlockSpec((None, None, n_heads, bs, HEAD_DIM), lambda b, r, n: (b, r, 0, n, 0))
    kern = functools.partial(_attn_kernel, bs=bs, qb=qb, reach=reach, seq=s_len, n_heads=n_heads)
    return pl.pallas_call(
        kern,
        grid=(batch, dilation, s_len // bs),
        in_specs=[main, prev, main, nxt, prev, main, nxt],
        out_specs=[out_spec, pl.BlockSpec((None, None, bs, LANES), lambda b, r, n: (b, r, n, 0))],
        out_shape=[jax.ShapeDtypeStruct((batch, dilation, n_heads, s_len, HEAD_DIM), F32),
                   jax.ShapeDtypeStruct((batch, dilation, s_len, LANES), F32)],
        scratch_shapes=[pltpu.VMEM((bs + 2 * reach, w), BF16), pltpu.VMEM((bs + 2 * reach, w), BF16)],
        compiler_params=_cparams(3),
        name=f"dilated_attn_g{group}",
    )(q, k, k, k, v, v, v)


def _attn_mix_kernel(*refs, dilations, bt):
    n_groups = len(dilations)
    o_refs = refs[:n_groups]
    l_refs = refs[n_groups:2 * n_groups]
    y_ref = refs[2 * n_groups]
    scr = refs[2 * n_groups + 1]
    n_heads = scr.shape[0]
    dmax = max(dilations)
    rows = bt // dmax
    for r in range(dmax):
        lse_tiles = [ref[r % d, pl.ds(r // d, rows, stride=dmax // d), :] for ref, d in zip(l_refs, dilations)]
        for h in range(n_heads):
            lane = h * (LANES // n_heads)
            lses = [t[:, lane:lane + 1] for t in lse_tiles]
            mx = functools.reduce(jnp.maximum, lses)
            es = [jnp.exp(l - mx) for l in lses]
            inv = 1.0 / functools.reduce(lambda a, b: a + b, es)
            outs = [ref[r % d, h, pl.ds(r // d, rows, stride=dmax // d), :] for ref, d in zip(o_refs, dilations)]
            num = functools.reduce(lambda a, b: a + b, [(e * inv) * o for e, o in zip(es, outs)])
            scr[h, pl.ds(r, rows, stride=dmax), :] = num
    for h in range(n_heads):
        y_ref[:, h * HEAD_DIM:(h + 1) * HEAD_DIM] = scr[h].astype(y_ref.dtype)


def _attn_mix(outs, lses, dilations, seq_len):
    batch, _, n_heads, _, hd = outs[0].shape
    dmax = max(dilations)
    bt = _tile(seq_len, 512)
    assert bt % (8 * dmax) == 0 and all(dmax % d == 0 for d in dilations)
    nbt = seq_len // bt
    specs = [pl.BlockSpec((None, d, n_heads, bt // d, hd), lambda b, i: (b, 0, 0, i, 0)) for d in dilations]
    lse_specs = [pl.BlockSpec((None, d, bt // d, LANES), lambda b, i: (b, 0, i, 0)) for d in dilations]
    return pl.pallas_call(
        functools.partial(_attn_mix_kernel, dilations=tuple(dilations), bt=bt),
        grid=(batch, nbt),
        in_specs=specs + lse_specs,
        out_specs=pl.BlockSpec((bt, n_heads * hd), lambda b, i: (b * nbt + i, 0)),
        out_shape=jax.ShapeDtypeStruct((batch * seq_len, n_heads * hd), BF16),
        scratch_shapes=[pltpu.VMEM((n_heads, bt, hd), F32)],
        compiler_params=_cparams(2),
        name="attn_mix",
    )(*outs, *lses)


def _conv_kernel(xp_ref, xc_ref, xn_ref, w_ref, b_ref, s_ref, o_ref, xe_ref, *, bt, n_tblk, ksize, halo):
    t = pl.program_id(1)
    left = (ksize - 1) // 2
    prev = xp_ref[...].astype(F32)
    nxt = xn_ref[...].astype(F32)
    xe_ref[0:halo, :] = jnp.where(t > 0, prev, 0.0)
    xe_ref[halo:halo + bt, :] = xc_ref[...].astype(F32)
    xe_ref[halo + bt:, :] = jnp.where(t < n_tblk - 1, nxt, 0.0)
    acc = xe_ref[pl.ds(halo - left, bt), :] * w_ref[0:1, :]
    for j in range(1, ksize):
        acc = acc + xe_ref[pl.ds(halo - left + j, bt), :] * w_ref[j:j + 1, :]
    y = acc + b_ref[...]
    o_ref[...] = (y * _sigmoid(y) * s_ref[...]).astype(o_ref.dtype)


def _conv_silu(x, w, b, scale, batch, seq_len):
    m, c = x.shape
    ksize = w.shape[0]
    halo = 8
    bt = _tile(seq_len, 512)
    bc = _tile(c, 512)
    n_tblk = seq_len // bt
    hb = bt // halo
    n_hblk = seq_len // halo
    x3 = x.reshape(batch, seq_len, c)
    cur = pl.BlockSpec((None, bt, bc), lambda b_, t, j: (b_, t, j))
    prev = pl.BlockSpec((None, halo, bc), lambda b_, t, j: (b_, jnp.maximum(t * hb - 1, 0), j))
    nxt = pl.BlockSpec((None, halo, bc), lambda b_, t, j: (b_, jnp.minimum((t + 1) * hb, n_hblk - 1), j))
    wspec = pl.BlockSpec((ksize, bc), lambda b_, t, j: (0, j))
    rspec = pl.BlockSpec((1, bc), lambda b_, t, j: (0, j))
    out = pl.pallas_call(
        functools.partial(_conv_kernel, bt=bt, n_tblk=n_tblk, ksize=ksize, halo=halo),
        grid=(batch, n_tblk, c // bc),
        in_specs=[prev, cur, nxt, wspec, rspec, rspec],
        out_specs=cur,
        out_shape=jax.ShapeDtypeStruct((batch, seq_len, c), BF16),
        scratch_shapes=[pltpu.VMEM((bt + 2 * halo, bc), F32)],
        compiler_params=_cparams(3),
        name="mlstm_conv_silu",
    )(x3, x3, x3, w, b.reshape(1, c), scale.reshape(1, c))
    return out.reshape(m, c)


def _gate_prep_kernel(g_ref, b_ref, o_ref, *, n_heads, chunk):
    x = g_ref[...] + b_ref[...]
    h = n_heads
    t = x.shape[1]

    def log_sigmoid(z):
        return jnp.minimum(z, 0.0) - jnp.log(1.0 + jnp.exp(-jnp.abs(z)))

    pos = lax.broadcasted_iota(jnp.int32, (h, t), 1) & (chunk - 1)

    def chunk_cumsum(z, reverse):
        shift = 1
        while shift < chunk:
            if reverse:
                moved = pltpu.roll(z, t - shift, 1)
                keep = pos < chunk - shift
            else:
                moved = pltpu.roll(z, shift, 1)
                keep = pos >= shift
            z = z + jnp.where(keep, moved, 0.0)
            shift *= 2
        return z

    o_ref[0] = chunk_cumsum(log_sigmoid(x[h:2 * h]), False)
    o_ref[1] = x[0:h]
    o_ref[2] = chunk_cumsum(log_sigmoid(x[3 * h:4 * h]), True)
    o_ref[3] = x[2 * h:3 * h]


def _gate_prep(gates_t, bias, n_heads, batch):
    rows, m = gates_t.shape
    t = m // batch
    return pl.pallas_call(
        functools.partial(_gate_prep_kernel, n_heads=n_heads, chunk=MLSTM_CHUNK),
        grid=(batch,),
        in_specs=[pl.BlockSpec((rows, t), lambda b: (0, b)), pl.BlockSpec((rows, 1), lambda b: (0, 0))],
        out_specs=pl.BlockSpec((None, 4, n_heads, t), lambda b: (b, 0, 0, 0)),
        out_shape=jax.ShapeDtypeStruct((batch, 4, n_heads, t), F32),
        compiler_params=_cparams(1),
        name="mlstm_gate_prep",
    )(gates_t, bias.reshape(rows, 1))


def _mlstm_direction(q_ref, k_ref, v_ref, grow_ref, gcol_ref, h_ref, c_ref, n_ref, m_ref, *, reverse, kind):
    L = q_ref.shape[0]
    q = q_ref[...]
    k = k_ref[...]
    b_row = grow_ref[kind:kind + 1, :]
    i_row = grow_ref[kind + 1:kind + 2, :]
    b_col = gcol_ref[:, kind:kind + 1]
    i_col = gcol_ref[:, kind + 1:kind + 2]
    m_prev = m_ref[...]
    li = lax.broadcasted_iota(jnp.int32, (L, L), 0)
    si = lax.broadcasted_iota(jnp.int32, (L, L), 1)
    visible = (si >= li) if reverse else (si <= li)
    dmat = jnp.where(visible, b_col - b_row + i_row, NEG_INF)
    row_max = jnp.max(dmat, axis=1, keepdims=True)
    s_qk = lax.dot_general(q, k, (((1,), (1,)), ((), ())), preferred_element_type=F32)
    b_last = b_col[0:1, :] if reverse else b_col[L - 1:L, :]
    g = b_last - b_col + i_col
    g_max = jnp.max(g, axis=0, keepdims=True)
    yield

    inter = b_col + m_prev
    m_t = jnp.maximum(inter, row_max)
    w_inter = jnp.exp(inter - m_t)
    qk = s_qk * jnp.exp(dmat - m_t)
    m_new = jnp.maximum(b_last + m_prev, g_max)
    decay = jnp.exp(b_last + m_prev - m_new)
    kw = k.astype(F32) * jnp.exp(g - m_new)
    yield

    v = v_ref[...]
    c_prev = c_ref[...]
    n_prev = n_ref[...]
    inter_num = jnp.dot(q, c_prev.astype(BF16), preferred_element_type=F32)
    intra_num = jnp.dot(qk.astype(BF16), v, preferred_element_type=F32)
    c_add = lax.dot_general(kw.astype(BF16), v, (((0,), (0,)), ((), ())), preferred_element_type=F32)
    qn = jnp.sum(q.astype(F32) * n_prev, axis=1, keepdims=True)
    den = w_inter * qn + jnp.sum(qk, axis=1, keepdims=True)
    yield

    h_ref[...] = (w_inter * inter_num + intra_num) / jnp.maximum(jnp.abs(den), jnp.exp(-m_t))
    c_ref[...] = decay * c_prev + c_add
    n_ref[...] = decay * n_prev + jnp.sum(kw, axis=0, keepdims=True)
    m_ref[...] = m_new
    yield


def _mlstm_kernel(qf, kf, vf, growf, gcolf, qb, kb, vb, growb, gcolb, hf_ref, hb_ref,
                  cf, nf, mf, cb, nb, mb, *, heads_per_step):
    @pl.when(pl.program_id(2) == 0)
    def _():
        for r in (cf, nf, mf, cb, nb, mb):
            r[...] = jnp.zeros(r.shape, r.dtype)

    d = qf.shape[1] // heads_per_step
    chains = []
    for hh in range(heads_per_step):
        cs = slice(hh * d, (hh + 1) * d)
        chains.append(_mlstm_direction(qf.at[:, cs], kf.at[:, cs], vf.at[:, cs], growf.at[hh], gcolf.at[hh],
                                       hf_ref.at[:, cs], cf.at[hh], nf.at[hh], mf.at[hh], reverse=False, kind=0))
        chains.append(_mlstm_direction(qb.at[:, cs], kb.at[:, cs], vb.at[:, cs], growb.at[hh], gcolb.at[hh],
                                       hb_ref.at[:, cs], cb.at[hh], nb.at[hh], mb.at[hh], reverse=True, kind=2))
    for _ in range(4):
        for chain in chains:
            next(chain)


def _mlstm_scan(qk, vo, grow, gcol, batch, seq_len, n_heads):
    m, width2 = qk.shape
    width = width2 // 2
    d = width // n_heads
    L = MLSTM_CHUNK
    nc = seq_len // L
    hps = MLSTM_HEADS_PER_STEP if n_heads % MLSTM_HEADS_PER_STEP == 0 else 1
    n_hblk = n_heads // hps
    qk3 = qk.reshape(batch, seq_len, width2)
    vo3 = vo.reshape(batch, seq_len, width2)

    def seq_spec(rev, col0):
        return pl.BlockSpec((None, L, hps * d), lambda b, h, c: (b, nc - 1 - c if rev else c, h + col0))

    def row_spec(rev):
        return pl.BlockSpec((None, hps, 4, L), lambda b, h, c: (b, h, 0, nc - 1 - c if rev else c))

    def col_spec(rev):
        return pl.BlockSpec((None, hps, L, 4), lambda b, h, c: (b, h, nc - 1 - c if rev else c, 0))

    def direction(rev):
        return [seq_spec(rev, 0), seq_spec(rev, n_hblk), seq_spec(rev, 0), row_spec(rev), col_spec(rev)]

    hf, hb = pl.pallas_call(
        functools.partial(_mlstm_kernel, heads_per_step=hps),
        grid=(batch, n_hblk, nc),
        in_specs=direction(False) + direction(True),
        out_specs=[seq_spec(False, 0), seq_spec(True, 0)],
        out_shape=[jax.ShapeDtypeStruct((batch, seq_len, width), F32)] * 2,
        scratch_shapes=[pltpu.VMEM((hps, d, d), F32), pltpu.VMEM((hps, 1, d), F32), pltpu.VMEM((hps, 1, 1), F32)] * 2,
        compiler_params=_cparams(3),
        name="mlstm_scan",
    )(qk3, qk3, vo3, grow, gcol, qk3, qk3, vo3, grow, gcol)
    return hf.reshape(m, width), hb.reshape(m, width)


def _mlstm_post_kernel(hf_ref, hb_ref, o_ref, w_ref, y_ref, *, n_heads):
    d = hf_ref.shape[1] // n_heads
    for h in range(n_heads):
        cs = slice(h * d, (h + 1) * d)
        x = hf_ref[:, cs] + hb_ref[:, cs]
        mu = jnp.mean(x, axis=1, keepdims=True)
        xc = x - mu
        var = jnp.mean(xc * xc, axis=1, keepdims=True)
        y = xc * lax.rsqrt(var + NORM_EPS) * w_ref[:, cs]
        y_ref[:, cs] = (_sigmoid(o_ref[:, cs].astype(F32)) * y).astype(y_ref.dtype)


def _mlstm_post(hf, hb, vo, w, n_heads):
    m, width = hf.shape
    bm = _tile(m, 256)
    spec = pl.BlockSpec((bm, width), lambda i: (i, 0))
    return pl.pallas_call(
        functools.partial(_mlstm_post_kernel, n_heads=n_heads),
        grid=(m // bm,),
        in_specs=[spec, spec, pl.BlockSpec((bm, width), lambda i: (i, 1)), pl.BlockSpec((1, width), lambda i: (0, 0))],
        out_specs=spec,
        out_shape=jax.ShapeDtypeStruct((m, width), BF16),
        compiler_params=_cparams(1),
        name="mlstm_post",
    )(hf, hb, vo, w.reshape(1, width))


def _rope_tables(seq_len, scale):
    half = ROPE_DIM // 2
    inv_freq = ROPE_THETA ** (-jnp.arange(half, dtype=F32) * 2.0 / ROPE_DIM)
    ang = jnp.arange(seq_len).astype(F32)[:, None] * inv_freq[None, :]
    cos, sin = jnp.cos(ang), jnp.sin(ang)
    zeros = jnp.zeros((seq_len, half), F32)
    rest = HEAD_DIM - ROPE_DIM
    c = jnp.concatenate([cos, cos, jnp.ones((seq_len, rest), F32)], axis=1)
    s1 = jnp.concatenate([zeros, sin, jnp.zeros((seq_len, rest), F32)], axis=1)
    s2 = jnp.concatenate([-sin, zeros, jnp.zeros((seq_len, rest), F32)], axis=1)
    return [c * scale, s1 * scale, s2 * scale]


def _ffn(x, norm_g, w_gate, w_up, w_down):
    d, d_ff = w_gate.shape
    h = _rmsnorm(x, norm_g, BF16)
    (act,) = _mm([h], [(w_gate, 0, False), (w_up, 0, False)], [(0, 0), (0, 1)], [], _ep_swiglu, [BF16],
                 n=d_ff, bm=2048, bn=256, name="ffn_gate_up")
    (y,) = _mm([act], [(w_down.astype(BF16), 0, False)], [(0, 0)], [(x, "mn")],
               functools.partial(_ep_residual, 0.5), [F32], n=d, bm=512, bn=512, name="ffn_down", rows_inner=True)
    return y


def _mixers(x, batch, seq_len, mix_norm, w_in, conv_w, conv_b, gate_bias, head_norm, w_battn, w_bmlstm, w_out):
    m, d = x.shape
    attn_out = w_battn.shape[0]
    attn_w = len(ATTN_GROUPS) * attn_out
    mw = w_bmlstm.shape[0]
    n_mheads = gate_bias.shape[0] // 4
    mhd = mw // n_mheads
    n_gate = 4 * n_mheads
    offs = np.cumsum([0, attn_w, attn_w, attn_w, mw, mw, mw, mw, n_gate, d, d]).tolist()

    w_t = jnp.swapaxes(w_in, 0, 1).astype(BF16)

    h = _rmsnorm(x, mix_norm, BF16)
    (qkm,) = _mm([h], [(w_t, offs[3], True)], [(0, 0)], [], _ep_identity, [BF16], n=2 * mw, bm=1024, bn=1024,
                 name="proj_mlstm_qk")
    (vom,) = _mm([h], [(w_t, offs[5], True)], [(0, 0)], [], _ep_identity, [BF16], n=2 * mw, bm=1024, bn=1024,
                 name="proj_mlstm_vo")
    (gates_t,) = _mm([w_t[offs[7]:offs[8]]], [(h, 0, True)], [(0, 0)], [], _ep_identity, [F32], n=m, bm=n_gate,
                     bn=1024, name="proj_mlstm_gates")
    (bgate,) = _mm([h], [(w_t, offs[8], True)], [(0, 0)], [], _ep_sigmoid, [BF16], n=2 * d,
                   bm=1024, bn=1024, name="proj_branch_gates")

    q_tabs = _rope_tables(seq_len, HEAD_DIM ** -0.5)
    k_tabs = _rope_tables(seq_len, 1.0)
    outs, lses, dilations = [], [], []
    for g, (window, dilation) in enumerate(ATTN_GROUPS):
        c0 = g * attn_out
        qg = _attn_proj(h, w_t, offs[0] + c0, attn_out, q_tabs, batch, seq_len, dilation, f"proj_attn_q{g}")
        kg = _attn_proj(h, w_t, offs[1] + c0, attn_out, k_tabs, batch, seq_len, dilation, f"proj_attn_k{g}")
        vg = _attn_proj(h, w_t, offs[2] + c0, attn_out, [], batch, seq_len, dilation, f"proj_attn_v{g}")
        o, lse = _attn_group(qg, kg, vg, g, window // (2 * dilation))
        outs.append(o)
        lses.append(lse)
        dilations.append(dilation)
    y_attn = _attn_mix(outs, lses, dilations, seq_len)

    k_scale = jnp.concatenate([jnp.ones((mw,), F32), jnp.full((mw,), mhd ** -0.5, F32)])
    qk = _conv_silu(qkm, conv_w, conv_b, k_scale, batch, seq_len)
    gp = _gate_prep(gates_t, gate_bias, n_mheads, batch)
    grow = jnp.transpose(gp, (0, 2, 1, 3))
    gcol = jnp.transpose(gp, (0, 2, 3, 1))
    hf, hb = _mlstm_scan(qk, vom, grow, gcol, batch, seq_len, n_mheads)
    y_mlstm = _mlstm_post(hf, hb, vom, head_norm, n_mheads)

    (merged,) = _mm([y_attn, y_mlstm], [(w_battn.astype(BF16), 0, False), (w_bmlstm.astype(BF16), 0, False)],
                    [(0, 0), (1, 1)], [(bgate, ("mn", 0)), (bgate, ("mn", d))], _ep_merge, [BF16],
                    n=d, bm=1024, bn=1024, name="branch_merge")
    (y,) = _mm([merged], [(w_out.astype(BF16), 0, False)], [(0, 0)], [(x, "mn")],
               functools.partial(_ep_residual, 1.0), [F32], n=d, bm=1024, bn=1024, name="out_proj")
    return y


def kernel(x, ffn1_norm, ffn1_w_gate, ffn1_w_up, ffn1_w_down, mix_norm, w_in, mlstm_conv_w, mlstm_conv_b, mlstm_gate_bias, mlstm_head_norm, w_branch_attn, w_branch_mlstm, w_out, ffn2_norm, ffn2_w_gate, ffn2_w_up, ffn2_w_down, final_norm):
    batch, seq_len, d = x.shape
    h = x.reshape(batch * seq_len, d)
    for l in range(ffn1_norm.shape[0]):
        h = _ffn(h, ffn1_norm[l], ffn1_w_gate[l], ffn1_w_up[l], ffn1_w_down[l])
        h = _mixers(h, batch, seq_len, mix_norm[l], w_in[l], mlstm_conv_w[l], mlstm_conv_b[l], mlstm_gate_bias[l],
                    mlstm_head_norm[l], w_branch_attn[l], w_branch_mlstm[l], w_out[l])
        h = _ffn(h, ffn2_norm[l], ffn2_w_gate[l], ffn2_w_up[l], ffn2_w_down[l])
    return _rmsnorm(h, final_norm, F32).reshape(batch, seq_len, d)
```

```python
import functools

import numpy as np
import jax
import jax.numpy as jnp
from jax import lax
from jax.experimental import pallas as pl
from jax.experimental.pallas import tpu as pltpu

F32 = jnp.float32
BF16 = jnp.bfloat16

HEAD_DIM = 128
ATTN_GROUPS = ((128, 1), (512, 4), (2048, 16))
ROPE_DIM = HEAD_DIM // 4
ROPE_THETA = 500000.0
MLSTM_CHUNK = 128
CONV_ROW_BLOCK = 128
MLSTM_HEADS_PER_STEP = 8
NORM_EPS = 1e-6
NEG_INF = -1e30

LANES = 128
BF16_SUBLANE_ROWS = 16
VMEM_LIMIT_BYTES = 56 * 1024 * 1024


def _cparams(n_axes):
    return pltpu.CompilerParams(
        dimension_semantics=("arbitrary",) * n_axes, vmem_limit_bytes=VMEM_LIMIT_BYTES)


def _tile(n, pref):
    if n <= pref:
        return n
    t = (pref // LANES) * LANES
    while t >= LANES:
        if n % t == 0:
            return t
        t -= LANES
    return n


def _sigmoid(x):
    return 0.5 * jnp.tanh(0.5 * x) + 0.5


def _rmsnorm_kernel(x_ref, g_ref, o_ref):
    x = x_ref[...]
    ms = jnp.mean(x * x, axis=-1, keepdims=True)
    o_ref[...] = (x * lax.rsqrt(ms + NORM_EPS) * g_ref[...]).astype(o_ref.dtype)


def _rmsnorm(x, g, out_dtype):
    m, d = x.shape
    bm = _tile(m, 256)
    return pl.pallas_call(
        _rmsnorm_kernel,
        grid=(m // bm,),
        in_specs=[pl.BlockSpec((bm, d), lambda i: (i, 0)), pl.BlockSpec((1, d), lambda i: (0, 0))],
        out_specs=pl.BlockSpec((bm, d), lambda i: (i, 0)),
        out_shape=jax.ShapeDtypeStruct((m, d), out_dtype),
        compiler_params=_cparams(1),
        name="rmsnorm",
    )(x, g.reshape(1, d))


def _mm_kernel(*refs, n_lhs, rhs_t, pairs, n_extra, epilogue):
    lhs = refs[:n_lhs]
    n_rhs = len(rhs_t)
    rhs = refs[n_lhs:n_lhs + n_rhs]
    extras = refs[n_lhs + n_rhs:n_lhs + n_rhs + n_extra]
    outs = refs[n_lhs + n_rhs + n_extra:]
    accs = []
    for a, b in pairs:
        contract = (((1,), (1,)), ((), ())) if rhs_t[b] else (((1,), (0,)), ((), ()))
        accs.append(lax.dot_general(lhs[a][...], rhs[b][...].astype(BF16), contract, preferred_element_type=F32))
    res = epilogue(accs, [e[...] for e in extras])
    for o, r in zip(outs, res):
        o[...] = r.astype(o.dtype)


def _mm(lhs, rhs, pairs, extras, epilogue, out_dtypes, *, n, bm, bn, name, rows_inner=False):
    m = lhs[0].shape[0]
    bm = _tile(m, bm)
    bn = _tile(int(functools.reduce(np.gcd, [c0 for _, c0, t in rhs if not t], n)), bn)
    if rows_inner:
        grid = (n // bn, m // bm)
        ij = lambda f: (lambda j, i: f(i, j))
    else:
        grid = (m // bm, n // bn)
        ij = lambda f: f
    in_specs = [pl.BlockSpec((bm, l.shape[1]), ij(lambda i, j: (i, 0))) for l in lhs]
    for r, c0, transposed in rhs:
        if transposed:
            assert c0 % BF16_SUBLANE_ROWS == 0
            in_specs.append(pl.BlockSpec(
                (pl.Element(bn), pl.Element(r.shape[1])),
                ij(lambda i, j, c0=c0: (pl.multiple_of(j * bn + c0, BF16_SUBLANE_ROWS), 0))))
        else:
            in_specs.append(pl.BlockSpec((r.shape[0], bn), ij(lambda i, j, off=c0 // bn: (0, j + off))))
    for arr, kind in extras:
        if kind == "mn" or isinstance(kind, tuple):
            off = 0 if kind == "mn" else kind[1] // bn
            in_specs.append(pl.BlockSpec((bm, bn), ij(lambda i, j, off=off: (i, j + off))))
        else:
            in_specs.append(pl.BlockSpec((1, bn), ij(lambda i, j: (0, j))))
    kern = functools.partial(
        _mm_kernel, n_lhs=len(lhs), rhs_t=tuple(t for _, _, t in rhs), pairs=tuple(pairs),
        n_extra=len(extras), epilogue=epilogue)
    return pl.pallas_call(
        kern,
        grid=grid,
        in_specs=in_specs,
        out_specs=[pl.BlockSpec((bm, bn), ij(lambda i, j: (i, j))) for _ in out_dtypes],
        out_shape=[jax.ShapeDtypeStruct((m, n), dt) for dt in out_dtypes],
        compiler_params=_cparams(2),
        name=name,
    )(*lhs, *[r for r, _, _ in rhs], *[e for e, _ in extras])


def _ep_identity(accs, extras):
    return [accs[0]]


def _ep_sigmoid(accs, extras):
    return [_sigmoid(accs[0])]


def _ep_swiglu(accs, extras):
    g, u = accs
    return [g * _sigmoid(g) * u]


def _ep_rotary(accs, extras):
    acc = accs[0]
    c, s1, s2 = extras
    pieces = []
    for h in range(acc.shape[1] // HEAD_DIM):
        x = acc[:, h * HEAD_DIM:(h + 1) * HEAD_DIM]
        half = ROPE_DIM // 2
        pieces.append(x * c + pltpu.roll(x, half, 1) * s1 + pltpu.roll(x, HEAD_DIM - half, 1) * s2)
    return [jnp.concatenate(pieces, axis=1) if len(pieces) > 1 else pieces[0]]


def _ep_merge(accs, extras):
    ya, ym = accs
    ga, gm = extras
    return [ga.astype(F32) * ya + gm.astype(F32) * ym]


def _ep_residual(scale, accs, extras):
    return [extras[0] + scale * accs[0]]


def _attn_kernel(q_ref, kp_ref, kc_ref, kn_ref, vp_ref, vc_ref, vn_ref, o_ref, lse_ref, kx_ref, vx_ref,
                 *, bs, qb, reach, seq, n_heads):
    n = pl.program_id(2)
    kx_ref[0:reach, :] = kp_ref[...]
    kx_ref[reach:reach + bs, :] = kc_ref[...]
    kx_ref[reach + bs:, :] = kn_ref[...]
    vx_ref[0:reach, :] = vp_ref[...]
    vx_ref[reach:reach + bs, :] = vc_ref[...]
    vx_ref[reach + bs:, :] = vn_ref[...]

    kw = qb + 2 * reach
    row = lax.broadcasted_iota(jnp.int32, (qb, kw), 0)
    col = lax.broadcasted_iota(jnp.int32, (qb, kw), 1)
    delta = col - row

    def head_chain(h, r0, mask, lse_parts):
        cs = slice(h * HEAD_DIM, (h + 1) * HEAD_DIM)
        q = q_ref[pl.ds(r0, qb), cs]
        k = kx_ref[pl.ds(r0, kw), cs]
        s = lax.dot_general(q, k, (((1,), (1,)), ((), ())), preferred_element_type=F32)
        yield
        s = jnp.where(mask, s, NEG_INF)
        m = jnp.max(s, axis=1, keepdims=True)
        yield
        p = jnp.exp(s - m)
        l = jnp.sum(p, axis=1, keepdims=True)
        v = vx_ref[pl.ds(r0, kw), cs]
        o = jnp.dot(p.astype(BF16), v, preferred_element_type=F32)
        yield
        o_ref[h, pl.ds(r0, qb), :] = o * (1.0 / l)
        lse_parts[h] = m + jnp.log(l)
        yield

    def body(a, carry):
        r0 = pl.multiple_of(a * qb, qb)
        kpos = n * bs + r0 - reach + col
        mask = (delta >= 0) & (delta <= 2 * reach) & (kpos >= 0) & (kpos < seq)
        lse_parts = [None] * n_heads
        chains = [head_chain(h, r0, mask, lse_parts) for h in range(n_heads)]
        for _ in range(4):
            for chain in chains:
                next(chain)
        owner = lax.broadcasted_iota(jnp.int32, (qb, LANES), 1) // (LANES // n_heads)
        tile = jnp.broadcast_to(lse_parts[n_heads - 1], (qb, LANES))
        for h in range(n_heads - 2, -1, -1):
            tile = jnp.where(owner == h, lse_parts[h], tile)
        lse_ref[pl.ds(r0, qb), :] = tile
        return carry

    lax.fori_loop(0, bs // qb, body, 0)


def _attn_proj_kernel(*refs, rotary, dilation):
    if rotary:
        h_ref, w_ref, c_ref, s1_ref, s2_ref, o_ref = refs[:6]
    else:
        h_ref, w_ref, o_ref = refs[:3]
    acc = lax.dot_general(h_ref[...], w_ref[...], (((1,), (1,)), ((), ())), preferred_element_type=F32)
    if rotary:
        (acc,) = _ep_rotary([acc], [c_ref[...], s1_ref[...], s2_ref[...]])
    if dilation == 1:
        o_ref[...] = acc.astype(o_ref.dtype)
    else:
        scr = refs[-1]
        rows = acc.shape[0] // dilation
        for c in range(acc.shape[1] // LANES):
            scr[c] = acc[:, c * LANES:(c + 1) * LANES]
        for r in range(dilation):
            for c in range(acc.shape[1] // LANES):
                o_ref[r, :, c * LANES:(c + 1) * LANES] = scr[c, pl.ds(r, rows, stride=dilation), :].astype(o_ref.dtype)


def _attn_proj(h, w_t, c0, wg, tables, batch, seq_len, dilation, name):
    m, d_model = h.shape
    bm = _tile(seq_len, 1024)
    bn = _tile(int(np.gcd(wg, c0)), 1024)
    nbt = seq_len // bm
    assert bm % (dilation * 16) == 0 and bn % HEAD_DIM == 0
    in_specs = [pl.BlockSpec((bm, d_model), lambda i, j: (i, 0)),
                pl.BlockSpec((bn, d_model), lambda i, j: (j + c0 // bn, 0))]
    args = [h, w_t]
    for tab in tables:
        in_specs.append(pl.BlockSpec((bm, LANES), lambda i, j: (i % nbt, 0)))
        args.append(tab)
    if dilation == 1:
        out_spec = pl.BlockSpec((None, None, bm, bn), lambda i, j: (i // nbt, 0, i % nbt, j))
        scratch = []
    else:
        out_spec = pl.BlockSpec((None, dilation, bm // dilation, bn), lambda i, j: (i // nbt, 0, i % nbt, j))
        scratch = [pltpu.VMEM((bn // LANES, bm, LANES), F32)]
    return pl.pallas_call(
        functools.partial(_attn_proj_kernel, rotary=bool(tables), dilation=dilation),
        grid=(m // bm, wg // bn),
        in_specs=in_specs,
        out_specs=out_spec,
        out_shape=jax.ShapeDtypeStruct((batch, dilation, seq_len // dilation, wg), BF16),
        scratch_shapes=scratch,
        compiler_params=_cparams(2),
        name=name,
    )(*args)


def _attn_group(q, k, v, group, reach):
    batch, dilation, s_len, w = q.shape
    n_heads = w // HEAD_DIM
    qb = 2 * reach
    bs = min(512, s_len)
    assert s_len % bs == 0 and bs % qb == 0 and LANES % n_heads == 0
    hb = bs // reach
    n_hblk = s_len // reach

    main = pl.BlockSpec((None, None, bs, w), lambda b, r, n: (b, r, n, 0))
    prev = pl.BlockSpec((None, None, reach, w), lambda b, r, n: (b, r, jnp.maximum(n * hb - 1, 0), 0))
    nxt = pl.BlockSpec((None, None, reach, w), lambda b, r, n: (b, r, jnp.minimum((n + 1) * hb, n_hblk - 1), 0))
    out_spec = pl.BlockSpec((None, None, n_heads, bs, HEAD_DIM), lambda b, r, n: (b, r, 0, n, 0))
    kern = functools.partial(_attn_kernel, bs=bs, qb=qb, reach=reach, seq=s_len, n_heads=n_heads)
    return pl.pallas_call(
        kern,
        grid=(batch, dilation, s_len // bs),
        in_specs=[main, prev, main, nxt, prev, main, nxt],
        out_specs=[out_spec, pl.BlockSpec((None, None, bs, LANES), lambda b, r, n: (b, r, n, 0))],
        out_shape=[jax.ShapeDtypeStruct((batch, dilation, n_heads, s_len, HEAD_DIM), F32),
                   jax.ShapeDtypeStruct((batch, dilation, s_len, LANES), F32)],
        scratch_shapes=[pltpu.VMEM((bs + 2 * reach, w), BF16), pltpu.VMEM((bs + 2 * reach, w), BF16)],
        compiler_params=_cparams(3),
        name=f"dilated_attn_g{group}",
    )(q, k, k, k, v, v, v)


def _attn_mix_kernel(*refs, dilations, bt):
    n_groups = len(dilations)
    o_refs = refs[:n_groups]
    l_refs = refs[n_groups:2 * n_groups]
    y_ref = refs[2 * n_groups]
    scr = refs[2 * n_groups + 1]
    n_heads = scr.shape[0]
    dmax = max(dilations)
    rows = bt // dmax
    for r in range(dmax):
        lse_tiles = [ref[r % d, pl.ds(r // d, rows, stride=dmax // d), :] for ref, d in zip(l_refs, dilations)]
        for h in range(n_heads):
            lane = h * (LANES // n_heads)
            lses = [t[:, lane:lane + 1] for t in lse_tiles]
            mx = functools.reduce(jnp.maximum, lses)
            es = [jnp.exp(l - mx) for l in lses]
            inv = 1.0 / functools.reduce(lambda a, b: a + b, es)
            outs = [ref[r % d, h, pl.ds(r // d, rows, stride=dmax // d), :] for ref, d in zip(o_refs, dilations)]
            num = functools.reduce(lambda a, b: a + b, [(e * inv) * o for e, o in zip(es, outs)])
            scr[h, pl.ds(r, rows, stride=dmax), :] = num
    for h in range(n_heads):
        y_ref[:, h * HEAD_DIM:(h + 1) * HEAD_DIM] = scr[h].astype(y_ref.dtype)


def _attn_mix(outs, lses, dilations, seq_len):
    batch, _, n_heads, _, hd = outs[0].shape
    dmax = max(dilations)
    bt = _tile(seq_len, 512)
    assert bt % (8 * dmax) == 0 and all(dmax % d == 0 for d in dilations)
    nbt = seq_len // bt
    specs = [pl.BlockSpec((None, d, n_heads, bt // d, hd), lambda b, i: (b, 0, 0, i, 0)) for d in dilations]
    lse_specs = [pl.BlockSpec((None, d, bt // d, LANES), lambda b, i: (b, 0, i, 0)) for d in dilations]
    return pl.pallas_call(
        functools.partial(_attn_mix_kernel, dilations=tuple(dilations), bt=bt),
        grid=(batch, nbt),
        in_specs=specs + lse_specs,
        out_specs=pl.BlockSpec((bt, n_heads * hd), lambda b, i: (b * nbt + i, 0)),
        out_shape=jax.ShapeDtypeStruct((batch * seq_len, n_heads * hd), BF16),
        scratch_shapes=[pltpu.VMEM((n_heads, bt, hd), F32)],
        compiler_params=_cparams(2),
        name="attn_mix",
    )(*outs, *lses)


def _conv_kernel(xp_ref, xc_ref, xn_ref, w_ref, b_ref, s_ref, o_ref, xs_ref, *, bt, n_tblk, ksize, halo):
    t = pl.program_id(1)
    left = (ksize - 1) // 2
    pad = CONV_ROW_BLOCK // 2
    bc = xc_ref.shape[1]
    zeros = jnp.zeros((pad - halo, bc), xs_ref.dtype)
    xs_ref[0:pad - halo, :] = zeros
    xs_ref[pad - halo:pad, :] = jnp.where(t > 0, xp_ref[...], jnp.zeros_like(xp_ref[...]))
    xs_ref[pad:pad + bt, :] = xc_ref[...]
    xs_ref[pad + bt:pad + bt + halo, :] = jnp.where(t < n_tblk - 1, xn_ref[...], jnp.zeros_like(xn_ref[...]))
    xs_ref[pad + bt + halo:, :] = zeros
    rb = CONV_ROW_BLOCK
    shifts = [j - left for j in range(ksize) if j != left]
    row = lax.broadcasted_iota(jnp.int32, (len(shifts) * rb, 2 * rb), 0)
    col = lax.broadcasted_iota(jnp.int32, (len(shifts) * rb, 2 * rb), 1)
    want = row + pad
    for i, sh in enumerate(shifts):
        want = jnp.where(row >= i * rb, row - i * rb + pad + sh, want)
    sel = jnp.where(col == want, 1.0, 0.0).astype(BF16)
    for a in range(bt // rb):
        win = xs_ref[a * rb:a * rb + 2 * rb, :]
        moved = jnp.dot(sel, win, preferred_element_type=F32)
        acc = xs_ref[pad + a * rb:pad + (a + 1) * rb, :].astype(F32) * w_ref[left:left + 1, :]
        for i, sh in enumerate(shifts):
            acc = acc + moved[i * rb:(i + 1) * rb, :] * w_ref[left + sh:left + sh + 1, :]
        y = acc + b_ref[...]
        o_ref[a * rb:(a + 1) * rb, :] = (y * _sigmoid(y) * s_ref[...]).astype(o_ref.dtype)


def _conv_silu(x, w, b, scale, batch, seq_len):
    m, c = x.shape
    ksize = w.shape[0]
    halo = 8
    bt = _tile(seq_len, 512)
    bc = _tile(c, 512)
    n_tblk = seq_len // bt
    hb = bt // halo
    n_hblk = seq_len // halo
    x3 = x.reshape(batch, seq_len, c)
    cur = pl.BlockSpec((None, bt, bc), lambda b_, t, j: (b_, t, j))
    prev = pl.BlockSpec((None, halo, bc), lambda b_, t, j: (b_, jnp.maximum(t * hb - 1, 0), j))
    nxt = pl.BlockSpec((None, halo, bc), lambda b_, t, j: (b_, jnp.minimum((t + 1) * hb, n_hblk - 1), j))
    wspec = pl.BlockSpec((ksize, bc), lambda b_, t, j: (0, j))
    rspec = pl.BlockSpec((1, bc), lambda b_, t, j: (0, j))
    out = pl.pallas_call(
        functools.partial(_conv_kernel, bt=bt, n_tblk=n_tblk, ksize=ksize, halo=halo),
        grid=(batch, n_tblk, c // bc),
        in_specs=[prev, cur, nxt, wspec, rspec, rspec],
        out_specs=cur,
        out_shape=jax.ShapeDtypeStruct((batch, seq_len, c), BF16),
        scratch_shapes=[pltpu.VMEM((bt + CONV_ROW_BLOCK, bc), BF16)],
        compiler_params=_cparams(3),
        name="mlstm_conv_silu",
    )(x3, x3, x3, w, b.reshape(1, c), scale.reshape(1, c))
    return out.reshape(m, c)


def _gate_prep_kernel(g_ref, b_ref, o_ref, *, n_heads, chunk):
    x = g_ref[...] + b_ref[...]
    h = n_heads
    t = x.shape[1]

    def log_sigmoid(z):
        return jnp.minimum(z, 0.0) - jnp.log(1.0 + jnp.exp(-jnp.abs(z)))

    pos = lax.broadcasted_iota(jnp.int32, (h, t), 1) & (chunk - 1)

    def chunk_cumsum(z, reverse):
        shift = 1
        while shift < chunk:
            if reverse:
                moved = pltpu.roll(z, t - shift, 1)
                keep = pos < chunk - shift
            else:
                moved = pltpu.roll(z, shift, 1)
                keep = pos >= shift
            z = z + jnp.where(keep, moved, 0.0)
            shift *= 2
        return z

    o_ref[0] = chunk_cumsum(log_sigmoid(x[h:2 * h]), False)
    o_ref[1] = x[0:h]
    o_ref[2] = chunk_cumsum(log_sigmoid(x[3 * h:4 * h]), True)
    o_ref[3] = x[2 * h:3 * h]


def _gate_prep(gates_t, bias, n_heads, batch):
    rows, m = gates_t.shape
    t = m // batch
    return pl.pallas_call(
        functools.partial(_gate_prep_kernel, n_heads=n_heads, chunk=MLSTM_CHUNK),
        grid=(batch,),
        in_specs=[pl.BlockSpec((rows, t), lambda b: (0, b)), pl.BlockSpec((rows, 1), lambda b: (0, 0))],
        out_specs=pl.BlockSpec((None, 4, n_heads, t), lambda b: (b, 0, 0, 0)),
        out_shape=jax.ShapeDtypeStruct((batch, 4, n_heads, t), F32),
        compiler_params=_cparams(1),
        name="mlstm_gate_prep",
    )(gates_t, bias.reshape(rows, 1))


def _mlstm_direction(q_ref, k_ref, v_ref, grow_ref, gcol_ref, h_ref, c_ref, n_ref, m_ref, *, reverse, kind):
    L = q_ref.shape[0]
    q = q_ref[...]
    k = k_ref[...]
    b_row = grow_ref[kind:kind + 1, :]
    i_row = grow_ref[kind + 1:kind + 2, :]
    b_col = gcol_ref[:, kind:kind + 1]
    i_col = gcol_ref[:, kind + 1:kind + 2]
    m_prev = m_ref[...]
    li = lax.broadcasted_iota(jnp.int32, (L, L), 0)
    si = lax.broadcasted_iota(jnp.int32, (L, L), 1)
    visible = (si >= li) if reverse else (si <= li)
    dmat = jnp.where(visible, b_col - b_row + i_row, NEG_INF)
    row_max = jnp.max(dmat, axis=1, keepdims=True)
    s_qk = lax.dot_general(q, k, (((1,), (1,)), ((), ())), preferred_element_type=F32)
    b_last = b_col[0:1, :] if reverse else b_col[L - 1:L, :]
    g = b_last - b_col + i_col
    g_max = jnp.max(g, axis=0, keepdims=True)
    yield

    inter = b_col + m_prev
    m_t = jnp.maximum(inter, row_max)
    w_inter = jnp.exp(inter - m_t)
    qk = s_qk * jnp.exp(dmat - m_t)
    m_new = jnp.maximum(b_last + m_prev, g_max)
    decay = jnp.exp(b_last + m_prev - m_new)
    kw = k.astype(F32) * jnp.exp(g - m_new)
    yield

    v = v_ref[...]
    c_prev = c_ref[...]
    n_prev = n_ref[...]
    inter_num = jnp.dot(q, c_prev.astype(BF16), preferred_element_type=F32)
    intra_num = jnp.dot(qk.astype(BF16), v, preferred_element_type=F32)
    c_add = lax.dot_general(kw.astype(BF16), v, (((0,), (0,)), ((), ())), preferred_element_type=F32)
    qn = jnp.sum(q.astype(F32) * n_prev, axis=1, keepdims=True)
    den = w_inter * qn + jnp.sum(qk, axis=1, keepdims=True)
    yield

    h_ref[...] = (w_inter * inter_num + intra_num) / jnp.maximum(jnp.abs(den), jnp.exp(-m_t))
    c_ref[...] = decay * c_prev + c_add
    n_ref[...] = decay * n_prev + jnp.sum(kw, axis=0, keepdims=True)
    m_ref[...] = m_new
    yield


def _mlstm_kernel(qf, kf, vf, growf, gcolf, qb, kb, vb, growb, gcolb, hf_ref, hb_ref,
                  cf, nf, mf, cb, nb, mb, *, heads_per_step):
    @pl.when(pl.program_id(2) == 0)
    def _():
        for r in (cf, nf, mf, cb, nb, mb):
            r[...] = jnp.zeros(r.shape, r.dtype)

    d = qf.shape[1] // heads_per_step
    chains = []
    for hh in range(heads_per_step):
        cs = slice(hh * d, (hh + 1) * d)
        chains.append(_mlstm_direction(qf.at[:, cs], kf.at[:, cs], vf.at[:, cs], growf.at[hh], gcolf.at[hh],
                                       hf_ref.at[:, cs], cf.at[hh], nf.at[hh], mf.at[hh], reverse=False, kind=0))
        chains.append(_mlstm_direction(qb.at[:, cs], kb.at[:, cs], vb.at[:, cs], growb.at[hh], gcolb.at[hh],
                                       hb_ref.at[:, cs], cb.at[hh], nb.at[hh], mb.at[hh], reverse=True, kind=2))
    for _ in range(4):
        for chain in chains:
            next(chain)


def _mlstm_scan(qk, vo, grow, gcol, batch, seq_len, n_heads):
    m, width2 = qk.shape
    width = width2 // 2
    d = width // n_heads
    L = MLSTM_CHUNK
    nc = seq_len // L
    hps = MLSTM_HEADS_PER_STEP if n_heads % MLSTM_HEADS_PER_STEP == 0 else 1
    n_hblk = n_heads // hps
    qk3 = qk.reshape(batch, seq_len, width2)
    vo3 = vo.reshape(batch, seq_len, width2)

    def seq_spec(rev, col0):
        return pl.BlockSpec((None, L, hps * d), lambda b, h, c: (b, nc - 1 - c if rev else c, h + col0))

    def row_spec(rev):
        return pl.BlockSpec((None, hps, 4, L), lambda b, h, c: (b, h, 0, nc - 1 - c if rev else c))

    def col_spec(rev):
        return pl.BlockSpec((None, hps, L, 4), lambda b, h, c: (b, h, nc - 1 - c if rev else c, 0))

    def direction(rev):
        return [seq_spec(rev, 0), seq_spec(rev, n_hblk), seq_spec(rev, 0), row_spec(rev), col_spec(rev)]

    hf, hb = pl.pallas_call(
        functools.partial(_mlstm_kernel, heads_per_step=hps),
        grid=(batch, n_hblk, nc),
        in_specs=direction(False) + direction(True),
        out_specs=[seq_spec(False, 0), seq_spec(True, 0)],
        out_shape=[jax.ShapeDtypeStruct((batch, seq_len, width), F32)] * 2,
        scratch_shapes=[pltpu.VMEM((hps, d, d), F32), pltpu.VMEM((hps, 1, d), F32), pltpu.VMEM((hps, 1, 1), F32)] * 2,
        compiler_params=_cparams(3),
        name="mlstm_scan",
    )(qk3, qk3, vo3, grow, gcol, qk3, qk3, vo3, grow, gcol)
    return hf.reshape(m, width), hb.reshape(m, width)


def _mlstm_post_kernel(hf_ref, hb_ref, o_ref, w_ref, y_ref, *, n_heads):
    d = hf_ref.shape[1] // n_heads
    for h in range(n_heads):
        cs = slice(h * d, (h + 1) * d)
        x = hf_ref[:, cs] + hb_ref[:, cs]
        mu = jnp.mean(x, axis=1, keepdims=True)
        xc = x - mu
        var = jnp.mean(xc * xc, axis=1, keepdims=True)
        y = xc * lax.rsqrt(var + NORM_EPS) * w_ref[:, cs]
        y_ref[:, cs] = (_sigmoid(o_ref[:, cs].astype(F32)) * y).astype(y_ref.dtype)


def _mlstm_post(hf, hb, vo, w, n_heads):
    m, width = hf.shape
    bm = _tile(m, 256)
    spec = pl.BlockSpec((bm, width), lambda i: (i, 0))
    return pl.pallas_call(
        functools.partial(_mlstm_post_kernel, n_heads=n_heads),
        grid=(m // bm,),
        in_specs=[spec, spec, pl.BlockSpec((bm, width), lambda i: (i, 1)), pl.BlockSpec((1, width), lambda i: (0, 0))],
        out_specs=spec,
        out_shape=jax.ShapeDtypeStruct((m, width), BF16),
        compiler_params=_cparams(1),
        name="mlstm_post",
    )(hf, hb, vo, w.reshape(1, width))


def _rope_tables(seq_len, scale):
    half = ROPE_DIM // 2
    inv_freq = ROPE_THETA ** (-jnp.arange(half, dtype=F32) * 2.0 / ROPE_DIM)
    ang = jnp.arange(seq_len).astype(F32)[:, None] * inv_freq[None, :]
    cos, sin = jnp.cos(ang), jnp.sin(ang)
    zeros = jnp.zeros((seq_len, half), F32)
    rest = HEAD_DIM - ROPE_DIM
    c = jnp.concatenate([cos, cos, jnp.ones((seq_len, rest), F32)], axis=1)
    s1 = jnp.concatenate([zeros, sin, jnp.zeros((seq_len, rest), F32)], axis=1)
    s2 = jnp.concatenate([-sin, zeros, jnp.zeros((seq_len, rest), F32)], axis=1)
    return [c * scale, s1 * scale, s2 * scale]


def _ffn(x, norm_g, w_gate, w_up, w_down):
    d, d_ff = w_gate.shape
    h = _rmsnorm(x, norm_g, BF16)
    (act,) = _mm([h], [(w_gate, 0, False), (w_up, 0, False)], [(0, 0), (0, 1)], [], _ep_swiglu, [BF16],
                 n=d_ff, bm=2048, bn=256, name="ffn_gate_up")
    (y,) = _mm([act], [(w_down.astype(BF16), 0, False)], [(0, 0)], [(x, "mn")],
               functools.partial(_ep_residual, 0.5), [F32], n=d, bm=512, bn=512, name="ffn_down", rows_inner=True)
    return y


def _mixers(x, batch, seq_len, mix_norm, w_in, conv_w, conv_b, gate_bias, head_norm, w_battn, w_bmlstm, w_out):
    m, d = x.shape
    attn_out = w_battn.shape[0]
    attn_w = len(ATTN_GROUPS) * attn_out
    mw = w_bmlstm.shape[0]
    n_mheads = gate_bias.shape[0] // 4
    mhd = mw // n_mheads
    n_gate = 4 * n_mheads
    offs = np.cumsum([0, attn_w, attn_w, attn_w, mw, mw, mw, mw, n_gate, d, d]).tolist()

    w_t = jnp.swapaxes(w_in, 0, 1).astype(BF16)

    h = _rmsnorm(x, mix_norm, BF16)
    (qkm,) = _mm([h], [(w_t, offs[3], True)], [(0, 0)], [], _ep_identity, [BF16], n=2 * mw, bm=1024, bn=1024,
                 name="proj_mlstm_qk")
    (vom,) = _mm([h], [(w_t, offs[5], True)], [(0, 0)], [], _ep_identity, [BF16], n=2 * mw, bm=1024, bn=1024,
                 name="proj_mlstm_vo")
    (gates_t,) = _mm([w_t[offs[7]:offs[8]]], [(h, 0, True)], [(0, 0)], [], _ep_identity, [F32], n=m, bm=n_gate,
                     bn=1024, name="proj_mlstm_gates")
    (bgate,) = _mm([h], [(w_t, offs[8], True)], [(0, 0)], [], _ep_sigmoid, [BF16], n=2 * d,
                   bm=1024, bn=1024, name="proj_branch_gates")

    q_tabs = _rope_tables(seq_len, HEAD_DIM ** -0.5)
    k_tabs = _rope_tables(seq_len, 1.0)
    outs, lses, dilations = [], [], []
    for g, (window, dilation) in enumerate(ATTN_GROUPS):
        c0 = g * attn_out
        qg = _attn_proj(h, w_t, offs[0] + c0, attn_out, q_tabs, batch, seq_len, dilation, f"proj_attn_q{g}")
        kg = _attn_proj(h, w_t, offs[1] + c0, attn_out, k_tabs, batch, seq_len, dilation, f"proj_attn_k{g}")
        vg = _attn_proj(h, w_t, offs[2] + c0, attn_out, [], batch, seq_len, dilation, f"proj_attn_v{g}")
        o, lse = _attn_group(qg, kg, vg, g, window // (2 * dilation))
        outs.append(o)
        lses.append(lse)
        dilations.append(dilation)
    y_attn = _attn_mix(outs, lses, dilations, seq_len)

    k_scale = jnp.concatenate([jnp.ones((mw,), F32), jnp.full((mw,), mhd ** -0.5, F32)])
    qk = _conv_silu(qkm, conv_w, conv_b, k_scale, batch, seq_len)
    gp = _gate_prep(gates_t, gate_bias, n_mheads, batch)
    grow = jnp.transpose(gp, (0, 2, 1, 3))
    gcol = jnp.transpose(gp, (0, 2, 3, 1))
    hf, hb = _mlstm_scan(qk, vom, grow, gcol, batch, seq_len, n_mheads)
    y_mlstm = _mlstm_post(hf, hb, vom, head_norm, n_mheads)

    (merged,) = _mm([y_attn, y_mlstm], [(w_battn.astype(BF16), 0, False), (w_bmlstm.astype(BF16), 0, False)],
                    [(0, 0), (1, 1)], [(bgate, ("mn", 0)), (bgate, ("mn", d))], _ep_merge, [BF16],
                    n=d, bm=1024, bn=1024, name="branch_merge")
    (y,) = _mm([merged], [(w_out.astype(BF16), 0, False)], [(0, 0)], [(x, "mn")],
               functools.partial(_ep_residual, 1.0), [F32], n=d, bm=1024, bn=1024, name="out_proj")
    return y


def kernel(x, ffn1_norm, ffn1_w_gate, ffn1_w_up, ffn1_w_down, mix_norm, w_in, mlstm_conv_w, mlstm_conv_b, mlstm_gate_bias, mlstm_head_norm, w_branch_attn, w_branch_mlstm, w_out, ffn2_norm, ffn2_w_gate, ffn2_w_up, ffn2_w_down, final_norm):
    batch, seq_len, d = x.shape
    h = x.reshape(batch * seq_len, d)
    for l in range(ffn1_norm.shape[0]):
        h = _ffn(h, ffn1_norm[l], ffn1_w_gate[l], ffn1_w_up[l], ffn1_w_down[l])
        h = _mixers(h, batch, seq_len, mix_norm[l], w_in[l], mlstm_conv_w[l], mlstm_conv_b[l], mlstm_gate_bias[l],
                    mlstm_head_norm[l], w_branch_attn[l], w_branch_mlstm[l], w_out[l])
        h = _ffn(h, ffn2_norm[l], ffn2_w_gate[l], ffn2_w_up[l], ffn2_w_down[l])
    return _rmsnorm(h, final_norm, F32).reshape(batch, seq_len, d)
```
